```python
import math
import jax, jax.numpy as jnp
from jax import lax
import numpy as np

D_MODEL = 2048
BATCH = 1
SEQ = 8192
DEPTH = 1
DEC_BATCH = 32
DEC_SEQ = 8
PAST_LEN = 16384
PAGE_SIZE = 128

POOL_WIDTH = D_MODEL // 2
POOL_WINDOWS = (2, 4, 8, 16)
N_POOL_GROUPS = len(POOL_WINDOWS)
POOL_GROUP = POOL_WIDTH // N_POOL_GROUPS
POOL_HIST = max(POOL_WINDOWS)
POOL_BUF = POOL_HIST - 1
HEAD_DIM = 128
ATTN_WIDTH = D_MODEL // 2
N_HEADS = ATTN_WIDTH // HEAD_DIM
Q_BLOCK = 128
SB_BIAS_HI = -4.0
SB_BIAS_LO = -10.0
N_BRANCHES = 2
IN_WIDTH = POOL_WIDTH + 3 * ATTN_WIDTH + N_BRANCHES * D_MODEL
N_EXPERTS = 64
TOP_K = 8
N_EXPERT_GROUPS = 8
TOPK_GROUPS = 4
EXPERTS_PER_GROUP = N_EXPERTS // N_EXPERT_GROUPS
EXPERT_DIM = 512
SHARED_DIM = 512
ROUTED_SCALE = 2.5
MOE_BLOCK = 128
N_MOD = 6
RMS_EPS = 1e-6

kernel_name = 'pool_stickbreak_moe_adaln_decode_step'


def rmsnorm(x, g):
    xf = x.astype(jnp.float32)
    xf = xf * lax.rsqrt(jnp.mean(xf * xf, axis=-1, keepdims=True) + RMS_EPS)
    return (xf * g.astype(jnp.float32)).astype(x.dtype)


def multiscale_pool(u, hist, pos, pool_mix, pool_scale):
    b, t, _ = u.shape
    rows = jnp.concatenate([hist.astype(u.dtype), u], axis=1)
    cs = jnp.cumsum(rows.astype(jnp.float32), axis=1)
    uf = u.astype(jnp.float32)
    outs = []
    for g, w in enumerate(POOL_WINDOWS):
        lo, hi = g * POOL_GROUP, (g + 1) * POOL_GROUP
        win = cs[:, POOL_HIST:, lo:hi] - cs[:, POOL_HIST - w:POOL_HIST - w + t, lo:hi]
        cnt = jnp.minimum(pos + 1, w).astype(jnp.float32)[None, :, None]
        outs.append(win / cnt - uf[:, :, lo:hi])
    d = jnp.stack(outs, axis=2).astype(u.dtype)
    mixed = jnp.einsum('btgc,gce->btge', d, pool_mix).reshape(b, t, POOL_WIDTH)
    return mixed * pool_scale, rows[:, -POOL_BUF:]


def stick_breaking(q, k, v, q_pos, k_pos, sb_bias):
    z = jnp.einsum('bthd,bshd->bhts', q * (1.0 / math.sqrt(HEAD_DIM)), k).astype(jnp.float32)
    z = z + sb_bias.astype(jnp.float32)[None, :, None, None]
    causal = k_pos[None, :] < q_pos[:, None]
    log_keep = jnp.where(causal, jax.nn.log_sigmoid(-z), 0.0)
    suffix = lax.cumsum(log_keep, axis=3, reverse=True) - log_keep
    a = jnp.where(causal, jnp.exp(jax.nn.log_sigmoid(z) + suffix), 0.0)
    return jnp.einsum('bhts,bshd->bthd', a.astype(v.dtype), v)


def stick_breaking_prompt(q, k, v, sb_bias):
    b, s, h, d = q.shape
    nb = s // Q_BLOCK
    qb = q.reshape(b, nb, Q_BLOCK, h, d).swapaxes(0, 1)
    pos_b = jnp.arange(s).reshape(nb, Q_BLOCK)
    k_pos = jnp.arange(s)
    ob = lax.map(lambda a: stick_breaking(a[0], k, v, a[1], k_pos, sb_bias), (qb, pos_b))
    return ob.swapaxes(0, 1).reshape(b, s, h, d)


def swiglu(h, wg, wu, wd):
    return (jax.nn.silu(h @ wg) * (h @ wu)) @ wd


def moe_ffn(h, w_router, router_bias, w_e_gate, w_e_up, w_e_down, w_s_gate, w_s_up, w_s_down):
    n = h.shape[0]
    scores = jax.nn.sigmoid((h @ w_router).astype(jnp.float32))
    sel = scores + router_bias.astype(jnp.float32)
    grp_score = lax.top_k(sel.reshape(n, N_EXPERT_GROUPS, EXPERTS_PER_GROUP), 2)[0].sum(-1)
    _, gidx = lax.top_k(grp_score, TOPK_GROUPS)
    gmask = (gidx[..., None] == jnp.arange(N_EXPERT_GROUPS)).any(axis=1)
    emask = jnp.repeat(gmask, EXPERTS_PER_GROUP, axis=1)
    _, eidx = lax.top_k(jnp.where(emask, sel, -jnp.inf), TOP_K)
    wts = jnp.take_along_axis(scores, eidx, axis=1)
    wts = wts / jnp.sum(wts, axis=-1, keepdims=True) * ROUTED_SCALE
    nk = n * TOP_K
    n_blocks = -(-nk // MOE_BLOCK) + N_EXPERTS
    cap = n_blocks * MOE_BLOCK
    flat_e = eidx.reshape(-1)
    flat_tok = jnp.repeat(jnp.arange(n, dtype=jnp.int32), TOP_K)
    order = jnp.argsort(flat_e)
    se, stok, sw = flat_e[order], flat_tok[order], wts.reshape(-1)[order]
    counts = jnp.bincount(flat_e, length=N_EXPERTS)
    starts = jnp.cumsum(counts) - counts
    padded = (counts + MOE_BLOCK - 1) // MOE_BLOCK * MOE_BLOCK
    pends = jnp.cumsum(padded)
    pstarts = pends - padded
    dest = pstarts[se] + jnp.arange(nk) - starts[se]
    slot_tok = jnp.full((cap,), n, jnp.int32).at[dest].set(stok)
    slot_w = jnp.zeros((cap,), jnp.float32).at[dest].set(sw)
    block_start = jnp.arange(n_blocks) * MOE_BLOCK
    block_e = jnp.minimum(jnp.sum(pends[None, :] <= block_start[:, None], axis=1), N_EXPERTS - 1)
    h_pad = jnp.concatenate([h, jnp.zeros((1, h.shape[1]), h.dtype)], axis=0)
    xb = h_pad[slot_tok].reshape(n_blocks, MOE_BLOCK, h.shape[1])

    def expert_block(args):
        xe, e = args
        return swiglu(xe, w_e_gate[e], w_e_up[e], w_e_down[e])

    yb = lax.map(expert_block, (xb, block_e)).reshape(cap, h.shape[1])
    routed = jnp.zeros((n + 1, h.shape[1]), h.dtype).at[slot_tok].add(
        yb * slot_w[:, None].astype(h.dtype))[:n]
    return routed + swiglu(h, w_s_gate, w_s_up, w_s_down)


def decoder_layer(x, c, hist, k_past, v_past, past_len, norm1_g, norm2_g, w_ada, b_ada, w_in,
                  pool_mix, pool_scale, w_br_pool, w_br_attn, sb_bias, w_o, w_router, router_bias,
                  w_e_gate, w_e_up, w_e_down, w_s_gate, w_s_up, w_s_down):
    b, t, _ = x.shape
    mod = jax.nn.silu(c) @ w_ada + b_ada
    shift1, scale1, gate1, shift2, scale2, gate2 = jnp.split(mod[:, None, :], N_MOD, axis=-1)
    h = rmsnorm(x, norm1_g) * (1 + scale1) + shift1
    proj = h @ w_in
    u, q, k, v, gate_logits = jnp.split(
        proj, [POOL_WIDTH, POOL_WIDTH + ATTN_WIDTH, POOL_WIDTH + 2 * ATTN_WIDTH,
               POOL_WIDTH + 3 * ATTN_WIDTH], axis=-1)
    q = q.reshape(b, t, N_HEADS, HEAD_DIM)
    k = k.reshape(b, t, N_HEADS, HEAD_DIM)
    v = v.reshape(b, t, N_HEADS, HEAD_DIM)
    pos = past_len + jnp.arange(t)
    pool_out, pool_state = multiscale_pool(u, hist, pos, pool_mix, pool_scale)
    if k_past is None:
        attn = stick_breaking_prompt(q, k, v, sb_bias)
    else:
        k_all = jnp.concatenate([k_past.astype(k.dtype), k], axis=1)
        v_all = jnp.concatenate([v_past.astype(v.dtype), v], axis=1)
        attn = stick_breaking(q, k_all, v_all, pos, jnp.arange(past_len + t), sb_bias)
    gates = jax.nn.sigmoid(gate_logits.astype(jnp.float32)).astype(x.dtype)
    g_pool, g_attn = jnp.split(gates, N_BRANCHES, axis=-1)
    merged = g_pool * (pool_out @ w_br_pool) + g_attn * (attn.reshape(b, t, ATTN_WIDTH) @ w_br_attn)
    x = x + gate1 * (merged @ w_o)
    h2 = rmsnorm(x, norm2_g) * (1 + scale2) + shift2
    ffn = moe_ffn(h2.reshape(b * t, D_MODEL), w_router, router_bias, w_e_gate, w_e_up, w_e_down,
                  w_s_gate, w_s_up, w_s_down).reshape(b, t, D_MODEL)
    x = x + gate2 * ffn
    return x, pool_state, k, v


def setup_inputs(seed: int = 0) -> dict:
    key = jax.random.key(seed)
    ks = jax.random.split(key, 28)
    f32 = jnp.float32
    n_pages = PAST_LEN // PAGE_SIZE
    n_used = DEC_BATCH * n_pages
    n_pool_pages = n_used + max(1, n_used // 4)

    def nrm(k, shape, scale):
        return jax.random.normal(k, shape, f32) * scale

    page_table = jax.random.permutation(ks[7], n_pool_pages)[:n_used].reshape(
        DEC_BATCH, n_pages).astype(jnp.int32)
    sb_bias = (jnp.linspace(SB_BIAS_HI, SB_BIAS_LO, N_HEADS, dtype=f32)[None, :]
               + nrm(ks[27], (DEPTH, N_HEADS), 0.1))
    return {
        'x_prompt': nrm(ks[0], (BATCH, SEQ, D_MODEL), 1.0),
        'x_sample': nrm(ks[1], (DEC_BATCH, DEC_SEQ, D_MODEL), 1.0),
        'c_prompt': nrm(ks[2], (BATCH, D_MODEL), 1.0),
        'c_sample': nrm(ks[3], (DEC_BATCH, D_MODEL), 1.0),
        'state_pool': nrm(ks[4], (DEPTH, DEC_BATCH, POOL_BUF, POOL_WIDTH), 1.0),
        'cache_k': nrm(ks[5], (DEPTH, n_pool_pages, PAGE_SIZE, N_HEADS, HEAD_DIM), 1.0),
        'cache_v': nrm(ks[6], (DEPTH, n_pool_pages, PAGE_SIZE, N_HEADS, HEAD_DIM), 1.0),
        'page_table': page_table,
        'norm1_g': 1.0 + nrm(ks[8], (DEPTH, D_MODEL), 0.1),
        'norm2_g': 1.0 + nrm(ks[9], (DEPTH, D_MODEL), 0.1),
        'w_ada': nrm(ks[10], (DEPTH, D_MODEL, N_MOD * D_MODEL), 0.5 * D_MODEL ** -0.5),
        'b_ada': nrm(ks[11], (DEPTH, N_MOD * D_MODEL), 0.02),
        'w_in': nrm(ks[12], (DEPTH, D_MODEL, IN_WIDTH), D_MODEL ** -0.5),
        'pool_mix': nrm(ks[13], (DEPTH, N_POOL_GROUPS, POOL_GROUP, POOL_GROUP), POOL_GROUP ** -0.5),
        'pool_scale': 1.0 + nrm(ks[14], (DEPTH, POOL_WIDTH), 0.1),
        'w_br_pool': nrm(ks[15], (DEPTH, POOL_WIDTH, D_MODEL), POOL_WIDTH ** -0.5),
        'w_br_attn': nrm(ks[16], (DEPTH, ATTN_WIDTH, D_MODEL), ATTN_WIDTH ** -0.5),
        'sb_bias': sb_bias,
        'w_o': nrm(ks[17], (DEPTH, D_MODEL, D_MODEL), D_MODEL ** -0.5),
        'w_router': nrm(ks[18], (DEPTH, D_MODEL, N_EXPERTS), D_MODEL ** -0.5),
        'router_bias': nrm(ks[19], (DEPTH, N_EXPERTS), 0.01),
        'w_e_gate': nrm(ks[20], (DEPTH, N_EXPERTS, D_MODEL, EXPERT_DIM), D_MODEL ** -0.5),
        'w_e_up': nrm(ks[21], (DEPTH, N_EXPERTS, D_MODEL, EXPERT_DIM), D_MODEL ** -0.5),
        'w_e_down': nrm(ks[22], (DEPTH, N_EXPERTS, EXPERT_DIM, D_MODEL), EXPERT_DIM ** -0.5),
        'w_s_gate': nrm(ks[23], (DEPTH, D_MODEL, SHARED_DIM), D_MODEL ** -0.5),
        'w_s_up': nrm(ks[24], (DEPTH, D_MODEL, SHARED_DIM), D_MODEL ** -0.5),
        'w_s_down': nrm(ks[25], (DEPTH, SHARED_DIM, D_MODEL), SHARED_DIM ** -0.5),
        'final_g': 1.0 + nrm(ks[26], (D_MODEL,), 0.1),
    }


def reference(x_prompt, x_sample, c_prompt, c_sample, state_pool, cache_k, cache_v, page_table,
              norm1_g, norm2_g, w_ada, b_ada, w_in, pool_mix, pool_scale, w_br_pool, w_br_attn,
              sb_bias, w_o, w_router, router_bias, w_e_gate, w_e_up, w_e_down, w_s_gate, w_s_up,
              w_s_down, final_g):
    past_len = page_table.shape[1] * cache_k.shape[2]
    xp, xs = x_prompt, x_sample
    nb_p, nb_s = x_prompt.shape[0], x_sample.shape[0]
    pool_p, k_p, v_p, pool_s, k_s, v_s = [], [], [], [], [], []
    for l in range(DEPTH):
        params = (norm1_g[l], norm2_g[l], w_ada[l], b_ada[l], w_in[l], pool_mix[l], pool_scale[l],
                  w_br_pool[l], w_br_attn[l], sb_bias[l], w_o[l], w_router[l], router_bias[l],
                  w_e_gate[l], w_e_up[l], w_e_down[l], w_s_gate[l], w_s_up[l], w_s_down[l])
        hist_p = jnp.zeros((nb_p, POOL_HIST, POOL_WIDTH), xp.dtype)
        xp, sp, kp, vp = decoder_layer(xp, c_prompt, hist_p, None, None, 0, *params)
        hist_s = jnp.concatenate([jnp.zeros((nb_s, 1, POOL_WIDTH), xs.dtype),
                                  state_pool[l].astype(xs.dtype)], axis=1)
        k_past = cache_k[l][page_table].reshape(nb_s, past_len, N_HEADS, HEAD_DIM)
        v_past = cache_v[l][page_table].reshape(nb_s, past_len, N_HEADS, HEAD_DIM)
        xs, ss, ksn, vsn = decoder_layer(xs, c_sample, hist_s, k_past, v_past, past_len, *params)
        pool_p.append(sp)
        k_p.append(kp)
        v_p.append(vp)
        pool_s.append(ss)
        k_s.append(ksn)
        v_s.append(vsn)
    y_prompt = rmsnorm(xp, final_g)
    y_sample = rmsnorm(xs, final_g)
    return (y_prompt, y_sample, jnp.stack(pool_p), jnp.stack(k_p), jnp.stack(v_p),
            jnp.stack(pool_s), jnp.stack(k_s), jnp.stack(v_s))
```

```python
import functools
import math

import jax
import jax.numpy as jnp
from jax import lax
from jax.experimental import pallas as pl
from jax.experimental.pallas import tpu as pltpu

F32 = jnp.float32
BF16 = jnp.bfloat16

HEAD_DIM = 128
POOL_WINDOWS = (2, 4, 8, 16)
POOL_HIST = 16
N_EXPERT_GROUPS = 8
TOPK_GROUPS = 4
TOP_K = 8
ROUTED_SCALE = 2.5
RMS_EPS = 1e-6
N_MOD = 6

MOE_TM = 256
ATTN_TQ = 256
PAGES_PER_STEP = 4
MIB = 1024 * 1024


def _cparams(n_axes, vmem_mib):
    return pltpu.CompilerParams(dimension_semantics=("arbitrary",) * n_axes,
                                vmem_limit_bytes=vmem_mib * MIB)


def _silu(x):
    return x * jax.nn.sigmoid(x)


def _resident(shape):
    return pl.BlockSpec(shape, lambda *_: (0,) * len(shape), pipeline_mode=pl.Buffered(1))


def _ada_kernel(c_ref, w_ref, b_ref, o_ref):
    s = _silu(c_ref[...]).astype(BF16)
    o_ref[...] = jnp.dot(s, w_ref[...].astype(BF16), preferred_element_type=F32) + b_ref[...]


def _ada(c_all, w_ada, b_ada, tn=512):
    r, d = c_all.shape
    n = w_ada.shape[1]
    return pl.pallas_call(
        _ada_kernel,
        grid=(n // tn,),
        in_specs=[pl.BlockSpec((r, d), lambda j: (0, 0)),
                  pl.BlockSpec((d, tn), lambda j: (0, j)),
                  pl.BlockSpec((1, tn), lambda j: (0, j))],
        out_specs=pl.BlockSpec((r, tn), lambda j: (0, j)),
        out_shape=jax.ShapeDtypeStruct((r, n), F32),
        compiler_params=_cparams(1, 32),
        name="ada_mod",
    )(c_all, w_ada, b_ada.reshape(1, n))


def _rms(x, g):
    return x * lax.rsqrt(jnp.mean(x * x, axis=-1, keepdims=True) + RMS_EPS) * g


def _normmod_kernel(x_ref, g_ref, sc_ref, sh_ref, o_ref):
    xn = _rms(x_ref[...], g_ref[...])
    o_ref[...] = (xn * (1.0 + sc_ref[...]) + sh_ref[...]).astype(o_ref.dtype)


def _mod_spec(mod, tm, d):
    if mod.shape[0] == 1:
        return pl.BlockSpec((1, d), lambda i: (0, 0))
    return pl.BlockSpec((tm, d), lambda i: (i, 0))


def _normmod(x, g, scale, shift, tm):
    n, d = x.shape
    return pl.pallas_call(
        _normmod_kernel,
        grid=(n // tm,),
        in_specs=[pl.BlockSpec((tm, d), lambda i: (i, 0)),
                  pl.BlockSpec((1, d), lambda i: (0, 0)),
                  _mod_spec(scale, tm, d), _mod_spec(shift, tm, d)],
        out_specs=pl.BlockSpec((tm, d), lambda i: (i, 0)),
        out_shape=jax.ShapeDtypeStruct((n, d), BF16),
        compiler_params=_cparams(1, 32),
        name="norm_mod",
    )(x, g.reshape(1, d), scale, shift)


def _mm_kernel(a_ref, w_ref, *refs, epilogue, n_out):
    out_refs, wb_ref = refs[:n_out], refs[n_out]

    @pl.when(pl.program_id(1) == 0)
    def _():
        wb_ref[...] = w_ref[...].astype(BF16)

    acc = jnp.dot(a_ref[...], wb_ref[...], preferred_element_type=F32)
    for r, v in zip(out_refs, epilogue(acc)):
        r[...] = v.astype(r.dtype)


def _mm(a, w, col0, ncols, out_dtypes, epilogue, tm, tn, name):
    m, k = a.shape
    cb0 = col0 // tn
    out_spec = pl.BlockSpec((tm, tn), lambda j, i: (i, j))
    return pl.pallas_call(
        functools.partial(_mm_kernel, epilogue=epilogue, n_out=len(out_dtypes)),
        grid=(ncols // tn, m // tm),
        in_specs=[pl.BlockSpec((tm, k), lambda j, i: (i, 0)),
                  pl.BlockSpec((k, tn), lambda j, i: (0, cb0 + j))],
        out_specs=[out_spec] * len(out_dtypes),
        out_shape=[jax.ShapeDtypeStruct((m, ncols), dt) for dt in out_dtypes],
        scratch_shapes=[pltpu.VMEM((k, tn), BF16)],
        compiler_params=_cparams(2, 48),
        name=name,
    )(a, w)


def _pool_groups(win_fn, u_fn, pos, mix_ref, scale_ref, store_fn, gw):
    for g, w in enumerate(POOL_WINDOWS):
        lo = g * gw
        win = win_fn(0, lo)
        for dlt in range(1, w):
            win = win + win_fn(dlt, lo)
        cnt = jnp.minimum(pos + 1, w).astype(F32)
        dd = win / cnt - u_fn(lo)
        mixed = jnp.dot(dd.astype(BF16), mix_ref[g].astype(BF16), preferred_element_type=F32)
        store_fn(lo, mixed * scale_ref[:, lo:lo + gw])


def _pool_prompt_kernel(hist_ref, u_ref, mix_ref, scale_ref, o_ref, ext_ref, *, tm, gw):
    i = pl.program_id(0)
    ext_ref[0:POOL_HIST, :] = jnp.where(i == 0, 0.0, hist_ref[...])
    ext_ref[POOL_HIST:POOL_HIST + tm, :] = u_ref[...]
    pos = i * tm + lax.broadcasted_iota(jnp.int32, (tm, 1), 0)

    def store(lo, val):
        o_ref[:, lo:lo + gw] = val.astype(o_ref.dtype)

    _pool_groups(lambda dlt, lo: ext_ref[pl.ds(POOL_HIST - dlt, tm), lo:lo + gw],
                 lambda lo: u_ref[:, lo:lo + gw], pos, mix_ref, scale_ref, store, gw)


def _pool_prompt(u, pool_mix, pool_scale, tm=512):
    n, pw = u.shape
    ng, gw, _ = pool_mix.shape
    hb = tm // POOL_HIST
    return pl.pallas_call(
        functools.partial(_pool_prompt_kernel, tm=tm, gw=gw),
        grid=(n // tm,),
        in_specs=[pl.BlockSpec((POOL_HIST, pw), lambda i: (jnp.maximum(i * hb - 1, 0), 0)),
                  pl.BlockSpec((tm, pw), lambda i: (i, 0)),
                  pl.BlockSpec((ng, gw, gw), lambda i: (0, 0, 0)),
                  pl.BlockSpec((1, pw), lambda i: (0, 0))],
        out_specs=pl.BlockSpec((tm, pw), lambda i: (i, 0)),
        out_shape=jax.ShapeDtypeStruct((n, pw), BF16),
        scratch_shapes=[pltpu.VMEM((POOL_HIST + tm, pw), F32)],
        compiler_params=_cparams(1, 32),
        name="pool_prompt",
    )(u, u, pool_mix, pool_scale.reshape(1, pw))


def _pool_sample_kernel(hist_ref, u_ref, mix_ref, scale_ref, o_ref, ext_ref, *, nb, t, gw, past_len):
    ext_ref[:, 0:POOL_HIST, :] = hist_ref[...]
    ext_ref[:, POOL_HIST:POOL_HIST + t, :] = u_ref[...]
    pos = past_len + (lax.broadcasted_iota(jnp.int32, (nb * t, 1), 0) & (t - 1))

    def store(lo, val):
        o_ref[:, lo:lo + gw] = val.astype(o_ref.dtype)

    _pool_groups(lambda dlt, lo: ext_ref[:, pl.ds(POOL_HIST - dlt, t), lo:lo + gw].reshape(nb * t, gw),
                 lambda lo: u_ref[:, :, lo:lo + gw].reshape(nb * t, gw), pos, mix_ref, scale_ref,
                 store, gw)


def _pool_sample(u3, hist3, pool_mix, pool_scale, past_len):
    nb, t, pw = u3.shape
    ng, gw, _ = pool_mix.shape
    return pl.pallas_call(
        functools.partial(_pool_sample_kernel, nb=nb, t=t, gw=gw, past_len=past_len),
        grid=(1,),
        in_specs=[pl.BlockSpec((nb, POOL_HIST, pw), lambda i: (0, 0, 0)),
                  pl.BlockSpec((nb, t, pw), lambda i: (0, 0, 0)),
                  pl.BlockSpec((ng, gw, gw), lambda i: (0, 0, 0)),
                  pl.BlockSpec((1, pw), lambda i: (0, 0))],
        out_specs=pl.BlockSpec((nb * t, pw), lambda i: (0, 0)),
        out_shape=jax.ShapeDtypeStruct((nb * t, pw), BF16),
        scratch_shapes=[pltpu.VMEM((nb, POOL_HIST + t, pw), F32)],
        compiler_params=_cparams(1, 32),
        name="pool_sample",
    )(hist3, u3, pool_mix, pool_scale.reshape(1, pw))


def _log_terms(z):
    sp = jnp.maximum(z, 0.0) + jnp.log1p(jnp.exp(-jnp.abs(z)))
    return -sp, z - sp


def _suffix_matrix(nk, nrep):
    r = lax.broadcasted_iota(jnp.int32, (nk, nk + nrep), 0)
    c = lax.broadcasted_iota(jnp.int32, (nk, nk + nrep), 1)
    return jnp.where((r > c) | (c >= nk), 1.0, 0.0).astype(BF16)


def _attn_prompt_kernel(bias_ref, q_ref, k_ref, v_ref, o_ref, acc_ref, car_ref, *, tq):
    h = pl.program_id(0)
    i = pl.program_id(1)
    bias = bias_ref[h]
    q = q_ref[...]
    sufm = _suffix_matrix(tq, HEAD_DIM)
    row = lax.broadcasted_iota(jnp.int32, (tq, tq), 0)
    col = lax.broadcasted_iota(jnp.int32, (tq, tq), 1)
    causal = col < row

    def block(j, masked):
        start = pl.multiple_of(j * tq, tq)
        kb = k_ref[pl.ds(start, tq), :]
        vb = v_ref[pl.ds(start, tq), :]
        z = lax.dot_general(q, kb, (((1,), (1,)), ((), ())), preferred_element_type=F32) + bias
        lk, ls = _log_terms(z)
        if masked:
            lk = jnp.where(causal, lk, 0.0)
        r = jnp.dot(lk.astype(BF16), sufm, preferred_element_type=F32)
        car = car_ref[...]
        a = jnp.exp(ls + r[:, :tq] + jnp.concatenate([car] * (tq // HEAD_DIM), axis=1))
        if masked:
            a = jnp.where(causal, a, 0.0)
        acc_ref[...] += jnp.dot(a.astype(BF16), vb, preferred_element_type=F32)
        car_ref[...] = car + r[:, tq:]

    acc_ref[...] = jnp.zeros_like(acc_ref)
    car_ref[...] = jnp.zeros_like(car_ref)
    block(i, True)

    def body(s, c):
        block(i - 1 - s, False)
        return c

    lax.fori_loop(0, i, body, 0)
    o_ref[...] = acc_ref[...].astype(o_ref.dtype)


def _attn_prompt(qs, kb, vb, sb_bias, tq=ATTN_TQ):
    s, aw = qs.shape
    nh = aw // HEAD_DIM
    return pl.pallas_call(
        functools.partial(_attn_prompt_kernel, tq=tq),
        grid_spec=pltpu.PrefetchScalarGridSpec(
            num_scalar_prefetch=1,
            grid=(nh, s // tq),
            in_specs=[pl.BlockSpec((tq, HEAD_DIM), lambda h, i, b: (i, h)),
                      pl.BlockSpec((s, HEAD_DIM), lambda h, i, b: (0, h)),
                      pl.BlockSpec((s, HEAD_DIM), lambda h, i, b: (0, h))],
            out_specs=pl.BlockSpec((tq, HEAD_DIM), lambda h, i, b: (i, h)),
            scratch_shapes=[pltpu.VMEM((tq, HEAD_DIM), F32), pltpu.VMEM((tq, HEAD_DIM), F32)]),
        out_shape=jax.ShapeDtypeStruct((s, aw), BF16),
        compiler_params=_cparams(2, 32),
        name="attn_prompt",
    )(sb_bias, qs, kb, vb)


def _attn_sample_kernel(pt_ref, bias_ref, q_ref, kn_ref, vn_ref, *refs, pps, nh, t):
    k_refs, v_refs = refs[:pps], refs[pps:2 * pps]
    o_ref, acc_ref, car_ref, kpad_ref, vpad_ref = refs[2 * pps:]
    c = pl.program_id(1)
    ht = nh * t
    sufm = _suffix_matrix(HEAD_DIM, HEAD_DIM)
    qb = q_ref[...].astype(BF16)

    def process(kv_refs, masked):
        npg = len(kv_refs)
        zs = []
        for kr, _ in kv_refs:
            for h in range(nh):
                sl = slice(h * HEAD_DIM, (h + 1) * HEAD_DIM)
                z = lax.dot_general(qb[:, sl], kr[:, sl].astype(BF16), (((1,), (1,)), ((), ())),
                                    preferred_element_type=F32)
                zs.append(z + bias_ref[h])
        z = jnp.concatenate(zs, axis=0)
        lk, ls = _log_terms(z)
        if masked:
            kpos = lax.broadcasted_iota(jnp.int32, z.shape, 1)
            qpos = lax.broadcasted_iota(jnp.int32, z.shape, 0) & (t - 1)
            causal = kpos < qpos
            lk = jnp.where(causal, lk, 0.0)
        r = jnp.dot(lk.astype(BF16), sufm, preferred_element_type=F32)
        car = car_ref[...]
        for p, (_, vr) in enumerate(kv_refs):
            rows = slice(p * ht, (p + 1) * ht)
            a = jnp.exp(ls[rows] + r[rows, :HEAD_DIM] + car)
            if masked:
                a = jnp.where(causal[rows], a, 0.0)
            a = a.astype(BF16)
            car = car + r[rows, HEAD_DIM:]
            for h in range(nh):
                sl = slice(h * HEAD_DIM, (h + 1) * HEAD_DIM)
                acc_ref[:, sl] += jnp.dot(a[h * t:(h + 1) * t], vr[:, sl].astype(BF16),
                                          preferred_element_type=F32)
        car_ref[...] = car

    @pl.when(c == 0)
    def _():
        acc_ref[...] = jnp.zeros_like(acc_ref)
        car_ref[...] = jnp.zeros_like(car_ref)
        kpad_ref[...] = jnp.zeros_like(kpad_ref)
        vpad_ref[...] = jnp.zeros_like(vpad_ref)
        kpad_ref[0:t, :] = kn_ref[...]
        vpad_ref[0:t, :] = vn_ref[...]
        process([(kpad_ref, vpad_ref)], True)

    process(list(zip(k_refs, v_refs)), False)

    @pl.when(c == pl.num_programs(1) - 1)
    def _():
        o_ref[...] = acc_ref[...]


def _attn_sample(q, kn, vn, cache_k, cache_v, page_table, sb_bias, t, pps=PAGES_PER_STEP):
    nbt, aw = q.shape
    nb = nbt // t
    nh = aw // HEAD_DIM
    n_pages = page_table.shape[1]
    page = cache_k.shape[1]
    assert page == HEAD_DIM and n_pages % pps == 0

    def page_spec(i):
        return pl.BlockSpec((None, page, aw),
                            lambda b, c, pt, bias: (pt[b, n_pages - 1 - (c * pps + i)], 0, 0))

    row_spec = pl.BlockSpec((t, aw), lambda b, c, pt, bias: (b, 0))
    return pl.pallas_call(
        functools.partial(_attn_sample_kernel, pps=pps, nh=nh, t=t),
        grid_spec=pltpu.PrefetchScalarGridSpec(
            num_scalar_prefetch=2,
            grid=(nb, n_pages // pps),
            in_specs=[row_spec, row_spec, row_spec]
            + [page_spec(i) for i in range(pps)] + [page_spec(i) for i in range(pps)],
            out_specs=row_spec,
            scratch_shapes=[pltpu.VMEM((t, aw), F32), pltpu.VMEM((nh * t, HEAD_DIM), F32),
                            pltpu.VMEM((page, aw), F32), pltpu.VMEM((page, aw), F32)]),
        out_shape=jax.ShapeDtypeStruct((nbt, aw), F32),
        compiler_params=_cparams(2, 40),
        name="attn_sample",
    )(page_table, sb_bias, q, kn, vn, *([cache_k] * pps), *([cache_v] * pps))


def _merge_kernel(po_ref, at_ref, gt_ref, x_ref, g1_ref, n2_ref, sc_ref, sh_ref,
                  wbp_ref, wba_ref, wo_ref, x1_ref, h2_ref, *, d):
    bp = jnp.dot(po_ref[...], wbp_ref[...], preferred_element_type=F32)
    ba = jnp.dot(at_ref[...].astype(BF16), wba_ref[...], preferred_element_type=F32)
    merged = gt_ref[:, :d] * bp + gt_ref[:, d:] * ba
    x1 = x_ref[...] + g1_ref[...] * jnp.dot(merged.astype(BF16), wo_ref[...],
                                            preferred_element_type=F32)
    x1_ref[...] = x1
    h2_ref[...] = _rms(x1, n2_ref[...]) * (1.0 + sc_ref[...]) + sh_ref[...]


def _merge(pool_out, attn, gates, x, gate1, norm2_g, scale2, shift2, wbp, wba, wo, tm):
    n, d = x.shape
    pw, aw = pool_out.shape[1], attn.shape[1]
    row = lambda w: pl.BlockSpec((tm, w), lambda i: (i, 0))
    return pl.pallas_call(
        functools.partial(_merge_kernel, d=d),
        grid=(n // tm,),
        in_specs=[row(pw), row(aw), row(2 * d), row(d), _mod_spec(gate1, tm, d),
                  pl.BlockSpec((1, d), lambda i: (0, 0)), _mod_spec(scale2, tm, d),
                  _mod_spec(shift2, tm, d), _resident(wbp.shape), _resident(wba.shape),
                  _resident(wo.shape)],
        out_specs=[row(d), row(d)],
        out_shape=[jax.ShapeDtypeStruct((n, d), F32), jax.ShapeDtypeStruct((n, d), F32)],
        compiler_params=_cparams(1, 56),
        name="merge",
    )(pool_out, attn, gates, x, gate1, norm2_g.reshape(1, d), scale2, shift2, wbp, wba, wo)


def _first_max(v, ids, n):
    m = jnp.max(v, axis=0, keepdims=True)
    idx = jnp.min(jnp.where(v == m, ids, n), axis=0, keepdims=True)
    return m, idx


def _router_kernel(h_ref, wr_ref, rb_ref, ei_ref, wt_ref, *, ne, ng, tkg, topk):
    epg = ne // ng
    hb = h_ref[...].astype(BF16)
    logits = lax.dot_general(wr_ref[...], hb, (((1,), (1,)), ((), ())), preferred_element_type=F32)
    scores = jax.nn.sigmoid(logits)
    sel = scores + rb_ref[...]
    tm = sel.shape[1]
    neg = -jnp.inf
    fiota = lambda n: lax.broadcasted_iota(jnp.int32, (n, tm), 0).astype(F32)
    ids_g = fiota(epg)
    gs = []
    for g in range(ng):
        v = sel[g * epg:(g + 1) * epg, :]
        m1, i1 = _first_max(v, ids_g, float(epg))
        m2 = jnp.max(jnp.where(ids_g == i1, neg, v), axis=0, keepdims=True)
        gs.append(m1 + m2)
    gsc = jnp.concatenate(gs, axis=0)
    ids_n = fiota(ng)
    gmask = jnp.zeros((ng, tm), F32)
    for _ in range(tkg):
        _, gi = _first_max(gsc, ids_n, float(ng))
        hit = ids_n == gi
        gmask = jnp.where(hit, 1.0, gmask)
        gsc = jnp.where(hit, neg, gsc)
    emask = jnp.concatenate(
        [jnp.broadcast_to(gmask[g:g + 1, :], (epg, tm)) for g in range(ng)], axis=0)
    cand = jnp.where(emask > 0.5, sel, neg)
    ids_e = fiota(ne)
    eis, wts = [], []
    for _ in range(topk):
        _, ei = _first_max(cand, ids_e, float(ne))
        hit = ids_e == ei
        eis.append(ei)
        wts.append(jnp.sum(jnp.where(hit, scores, 0.0), axis=0, keepdims=True))
        cand = jnp.where(hit, neg, cand)
    w = jnp.concatenate(wts, axis=0)
    ei_ref[...] = jnp.concatenate(eis, axis=0).astype(jnp.int32)
    wt_ref[...] = w / jnp.sum(w, axis=0, keepdims=True) * ROUTED_SCALE


def _router(h2, wr_t, router_bias, tm):
    n, d = h2.shape
    ne = wr_t.shape[0]
    return pl.pallas_call(
        functools.partial(_router_kernel, ne=ne, ng=N_EXPERT_GROUPS, tkg=TOPK_GROUPS, topk=TOP_K),
        grid=(n // tm,),
        in_specs=[pl.BlockSpec((tm, d), lambda i: (i, 0)),
                  pl.BlockSpec((ne, d), lambda i: (0, 0)),
                  pl.BlockSpec((ne, 1), lambda i: (0, 0))],
        out_specs=[pl.BlockSpec((TOP_K, tm), lambda i: (0, i)),
                   pl.BlockSpec((TOP_K, tm), lambda i: (0, i))],
        out_shape=[jax.ShapeDtypeStruct((TOP_K, n), jnp.int32),
                   jax.ShapeDtypeStruct((TOP_K, n), F32)],
        compiler_params=_cparams(1, 32),
        name="router",
    )(h2, wr_t, router_bias.reshape(ne, 1))


def _expert_kernel(be_ref, nv_ref, tok_ref, dst_ref, hp_ref, hs_ref, wg_ref, wu_ref, wd_ref,
                   y_ref, xbuf, ybuf, gsem, ssem, *, tm, n_prompt):
    b = pl.program_id(0)
    nv = nv_ref[b]

    def gather(src, row, r):
        return pltpu.make_async_copy(src.at[pl.ds(row, 1)], xbuf.at[pl.ds(r, 1)], gsem)

    def scatter(r, row):
        return pltpu.make_async_copy(ybuf.at[pl.ds(r, 1)], y_ref.at[pl.ds(row, 1)], ssem)

    @pl.when(nv > 0)
    def _():
        @pl.when(nv < tm)
        def _():
            xbuf[...] = jnp.zeros_like(xbuf)

        def g_start(r, c):
            tok = tok_ref[0, 0, r]

            @pl.when(tok < n_prompt)
            def _():
                gather(hp_ref, tok, r).start()

            @pl.when(tok >= n_prompt)
            def _():
                gather(hs_ref, tok - n_prompt, r).start()

            return c

        lax.fori_loop(0, nv, g_start, 0)

        def g_wait(r, c):
            gather(hp_ref, 0, r).wait()
            return c

        lax.fori_loop(0, nv, g_wait, 0)

        x = xbuf[...].astype(BF16)
        gate = jnp.dot(x, wg_ref[...].astype(BF16), preferred_element_type=F32)
        up = jnp.dot(x, wu_ref[...].astype(BF16), preferred_element_type=F32)
        act = (_silu(gate) * up).astype(BF16)
        ybuf[...] = jnp.dot(act, wd_ref[...].astype(BF16), preferred_element_type=F32)

        def s_start(r, c):
            scatter(r, dst_ref[0, 0, r]).start()
            return c

        lax.fori_loop(0, nv, s_start, 0)

        def s_wait(r, c):
            scatter(r, 0).wait()
            return c

        lax.fori_loop(0, nv, s_wait, 0)


def _experts(block_e, block_nv, slot_tok, slot_dst, h2_p, h2_s, w_gate, w_up, w_down, tm=MOE_TM):
    nb = block_e.shape[0]
    n_prompt, d = h2_p.shape
    n_all = n_prompt + h2_s.shape[0]
    ed = w_gate.shape[2]
    smem_row = pl.BlockSpec((1, 1, tm), lambda b, be, nv: (b, 0, 0), memory_space=pltpu.SMEM)
    any_spec = pl.BlockSpec(memory_space=pl.ANY)
    return pl.pallas_call(
        functools.partial(_expert_kernel, tm=tm, n_prompt=n_prompt),
        grid_spec=pltpu.PrefetchScalarGridSpec(
            num_scalar_prefetch=2,
            grid=(nb,),
            in_specs=[smem_row, smem_row, any_spec, any_spec,
                      pl.BlockSpec((None, d, ed), lambda b, be, nv: (be[b], 0, 0)),
                      pl.BlockSpec((None, d, ed), lambda b, be, nv: (be[b], 0, 0)),
                      pl.BlockSpec((None, ed, d), lambda b, be, nv: (be[b], 0, 0))],
            out_specs=any_spec,
            scratch_shapes=[pltpu.VMEM((tm, d), F32), pltpu.VMEM((tm, d), F32),
                            pltpu.SemaphoreType.DMA(()), pltpu.SemaphoreType.DMA(())]),
        out_shape=jax.ShapeDtypeStruct((n_all * TOP_K, d), F32),
        compiler_params=_cparams(1, 56),
        name="experts",
    )(block_e, block_nv, slot_tok.reshape(nb, 1, tm), slot_dst.reshape(nb, 1, tm),
      h2_p, h2_s, w_gate, w_up, w_down)


def _dispatch_meta(eidx, n_experts, tm):
    n = eidx.shape[0]
    nk = n * TOP_K
    nb = -(-nk // tm) + n_experts
    flat_e = eidx.reshape(-1)
    order = jnp.argsort(flat_e, stable=True).astype(jnp.int32)
    counts = jnp.sum((flat_e[:, None] == jnp.arange(n_experts, dtype=jnp.int32)[None, :])
                     .astype(jnp.int32), axis=0)
    starts = jnp.cumsum(counts) - counts
    padded = (counts + tm - 1) // tm * tm
    pends = jnp.cumsum(padded)
    pstarts = pends - padded
    nb_used = pends[-1] // tm
    blk = jnp.arange(nb, dtype=jnp.int32)
    blk_start = blk * tm
    be = jnp.minimum(jnp.sum((pends[None, :] <= blk_start[:, None]).astype(jnp.int32), axis=1),
                     n_experts - 1)
    used = blk < nb_used
    be = jnp.where(used, be, be[jnp.maximum(nb_used - 1, 0)]).astype(jnp.int32)
    j0 = blk_start - pstarts[be]
    nv = jnp.where(used, jnp.clip(counts[be] - j0, 0, tm), 0).astype(jnp.int32)
    spos = starts[be][:, None] + j0[:, None] + jnp.arange(tm, dtype=jnp.int32)[None, :]
    flat_sorted = order[jnp.clip(spos, 0, nk - 1)]
    return be, nv, (flat_sorted // TOP_K).astype(jnp.int32), flat_sorted.astype(jnp.int32)


def _combine_kernel(y_ref, wt_ref, h_ref, x_ref, g2_ref, fg_ref, wgu_ref, wd_ref, o_ref, *, d, sd):
    wt = wt_ref[...]
    routed = y_ref[:, 0:d] * wt[:, 0:1]
    for k in range(1, TOP_K):
        routed = routed + y_ref[:, k * d:(k + 1) * d] * wt[:, k:k + 1]
    hb = h_ref[...].astype(BF16)
    gu = jnp.dot(hb, wgu_ref[...], preferred_element_type=F32)
    act = (_silu(gu[:, :sd]) * gu[:, sd:]).astype(BF16)
    ffn = routed + jnp.dot(act, wd_ref[...], preferred_element_type=F32)
    x2 = x_ref[...] + g2_ref[...] * ffn
    o_ref[...] = _rms(x2, fg_ref[...])


def _combine(y2, row0, wts, h2, x1, gate2, final_g, ws_gu, ws_d, tm):
    n, d = h2.shape
    sd = ws_d.shape[0]
    rb0 = row0 // tm
    row = lambda w: pl.BlockSpec((tm, w), lambda i: (i, 0))
    return pl.pallas_call(
        functools.partial(_combine_kernel, d=d, sd=sd),
        grid=(n // tm,),
        in_specs=[pl.BlockSpec((tm, TOP_K * d), lambda i: (rb0 + i, 0)), row(TOP_K), row(d), row(d),
                  _mod_spec(gate2, tm, d), pl.BlockSpec((1, d), lambda i: (0, 0)),
                  _resident(ws_gu.shape), _resident(ws_d.shape)],
        out_specs=row(d),
        out_shape=jax.ShapeDtypeStruct((n, d), F32),
        compiler_params=_cparams(1, 48),
        name="combine",
    )(y2, wts, h2, x1, gate2, final_g.reshape(1, d), ws_gu, ws_d)


def _in_proj(h, w_in, pw, aw, d, tm, q_dtype, want_bf16_kv):
    tn = 512
    ident = lambda acc: (acc,)
    (u,) = _mm(h, w_in, 0, pw, [F32], ident, tm, tn, "proj_u")
    qscale = 1.0 / math.sqrt(HEAD_DIM)
    (q,) = _mm(h, w_in, pw, aw, [q_dtype], lambda acc: (acc * qscale,), tm, tn, "proj_q")
    kv_dt = [F32, BF16] if want_bf16_kv else [F32]
    both = (lambda acc: (acc, acc)) if want_bf16_kv else ident
    k = _mm(h, w_in, pw + aw, aw, kv_dt, both, tm, tn, "proj_k")
    v = _mm(h, w_in, pw + 2 * aw, aw, kv_dt, both, tm, tn, "proj_v")
    (gates,) = _mm(h, w_in, pw + 3 * aw, 2 * d, [F32], lambda acc: (jax.nn.sigmoid(acc),),
                   tm, tn, "proj_gates")
    return u, q, k, v, gates


def kernel(x_prompt, x_sample, c_prompt, c_sample, state_pool, cache_k, cache_v, page_table,
           norm1_g, norm2_g, w_ada, b_ada, w_in, pool_mix, pool_scale, w_br_pool, w_br_attn,
           sb_bias, w_o, w_router, router_bias, w_e_gate, w_e_up, w_e_down, w_s_gate, w_s_up,
           w_s_down, final_g):
    depth = w_in.shape[0]
    nbp, seq, d = x_prompt.shape
    nbs, t, _ = x_sample.shape
    assert nbp == 1 and depth == 1 and t & (t - 1) == 0
    pw = pool_scale.shape[1]
    aw = w_br_attn.shape[1]
    n_experts = w_router.shape[2]
    n_pages, page = page_table.shape[1], cache_k.shape[2]
    past_len = n_pages * page
    ns = nbs * t
    buf = POOL_HIST - 1

    xp = x_prompt.reshape(seq, d)
    xs = x_sample.reshape(ns, d)
    rows = nbp + nbs
    rpad = -(-rows // 8) * 8
    c_all = jnp.concatenate([c_prompt, c_sample, jnp.zeros((rpad - rows, d), F32)], axis=0)

    outs = {k: [] for k in ("pool_p", "k_p", "v_p", "pool_s", "k_s", "v_s")}
    for l in range(depth):
        mod = _ada(c_all, w_ada[l], b_ada[l])
        mod_p = [mod[0:1, i * d:(i + 1) * d] for i in range(N_MOD)]
        mod_s = [jnp.repeat(mod[nbp:rows, i * d:(i + 1) * d], t, axis=0) for i in range(N_MOD)]
        wbp, wba, wo = (w.astype(BF16) for w in (w_br_pool[l], w_br_attn[l], w_o[l]))
        wr_t = w_router[l].T.astype(BF16)
        ws_gu = jnp.concatenate([w_s_gate[l], w_s_up[l]], axis=1).astype(BF16)
        ws_d = w_s_down[l].astype(BF16)

        hp = _normmod(xp, norm1_g[l], mod_p[1], mod_p[0], 512)
        u_p, q_p, (k_p, kb_p), (v_p, vb_p), gates_p = _in_proj(hp, w_in[l], pw, aw, d, 1024, BF16, True)
        pool_p = _pool_prompt(u_p, pool_mix[l], pool_scale[l])
        attn_p = _attn_prompt(q_p, kb_p, vb_p, sb_bias[l])
        x1_p, h2_p = _merge(pool_p, attn_p, gates_p, xp, mod_p[2], norm2_g[l], mod_p[4], mod_p[3],
                            wbp, wba, wo, 256)

        hs = _normmod(xs, norm1_g[l], mod_s[1], mod_s[0], ns)
        u_s, q_s, (k_s,), (v_s,), gates_s = _in_proj(hs, w_in[l], pw, aw, d, ns, F32, False)
        hist_s = jnp.concatenate([jnp.zeros((nbs, 1, pw), F32), state_pool[l]], axis=1)
        pool_s = _pool_sample(u_s.reshape(nbs, t, pw), hist_s, pool_mix[l], pool_scale[l], past_len)
        attn_s = _attn_sample(q_s, k_s, v_s, cache_k[l].reshape(-1, page, aw),
                              cache_v[l].reshape(-1, page, aw), page_table, sb_bias[l], t)
        x1_s, h2_s = _merge(pool_s, attn_s, gates_s, xs, mod_s[2], norm2_g[l], mod_s[4], mod_s[3],
                            wbp, wba, wo, ns)

        ei_p, wt_p = _router(h2_p, wr_t, router_bias[l], 1024)
        ei_s, wt_s = _router(h2_s, wr_t, router_bias[l], ns)
        eidx = jnp.concatenate([ei_p, ei_s], axis=1).T
        be, nv, slot_tok, slot_dst = _dispatch_meta(eidx, n_experts, MOE_TM)
        y = _experts(be, nv, slot_tok, slot_dst, h2_p, h2_s, w_e_gate[l], w_e_up[l], w_e_down[l])
        y2 = y.reshape(seq + ns, TOP_K * d)
        xp = _combine(y2, 0, wt_p.T, h2_p, x1_p, mod_p[5], final_g, ws_gu, ws_d, 128)
        xs = _combine(y2, seq, wt_s.T, h2_s, x1_s, mod_s[5], final_g, ws_gu, ws_d, 128)

        outs["pool_p"].append(u_p[seq - buf:].reshape(nbp, buf, pw))
        outs["k_p"].append(k_p.reshape(nbp, seq, aw // HEAD_DIM, HEAD_DIM))
        outs["v_p"].append(v_p.reshape(nbp, seq, aw // HEAD_DIM, HEAD_DIM))
        outs["pool_s"].append(jnp.concatenate(
            [state_pool[l][:, t:], u_s.reshape(nbs, t, pw)], axis=1))
        outs["k_s"].append(k_s.reshape(nbs, t, aw // HEAD_DIM, HEAD_DIM))
        outs["v_s"].append(v_s.reshape(nbs, t, aw // HEAD_DIM, HEAD_DIM))

    return (xp.reshape(nbp, seq, d), xs.reshape(nbs, t, d), jnp.stack(outs["pool_p"]),
            jnp.stack(outs["k_p"]), jnp.stack(outs["v_p"]), jnp.stack(outs["pool_s"]),
            jnp.stack(outs["k_s"]), jnp.stack(outs["v_s"]))
```

```python
import functools
import math

import jax
import jax.numpy as jnp
from jax import lax
from jax.experimental import pallas as pl
from jax.experimental.pallas import tpu as pltpu

F32 = jnp.float32
BF16 = jnp.bfloat16

HEAD_DIM = 128
POOL_WINDOWS = (2, 4, 8, 16)
POOL_HIST = 16
N_EXPERT_GROUPS = 8
TOPK_GROUPS = 4
TOP_K = 8
ROUTED_SCALE = 2.5
RMS_EPS = 1e-6
N_MOD = 6
LOG2E = 1.4426950408889634

MOE_TM = 256
MOE_DMA_UNROLL = 8
MOE_N_CHUNK = 2
ATTN_TQ = 512
ATTN_TK = 256
PAGES_PER_STEP = 8
PAGE_SKEW = 2
MIB = 1024 * 1024


def _cparams(n_axes, vmem_mib):
    return pltpu.CompilerParams(dimension_semantics=("arbitrary",) * n_axes,
                                vmem_limit_bytes=vmem_mib * MIB)


def _silu(x):
    return x * jax.nn.sigmoid(x)


def _resident(shape):
    return pl.BlockSpec(shape, lambda *_: (0,) * len(shape), pipeline_mode=pl.Buffered(1))


def _ada_kernel(c_ref, w_ref, b_ref, o_ref):
    s = _silu(c_ref[...]).astype(BF16)
    o_ref[...] = jnp.dot(s, w_ref[...].astype(BF16), preferred_element_type=F32) + b_ref[...]


def _ada(c_all, w_ada, b_ada, tn=512):
    r, d = c_all.shape
    n = w_ada.shape[1]
    return pl.pallas_call(
        _ada_kernel,
        grid=(n // tn,),
        in_specs=[pl.BlockSpec((r, d), lambda j: (0, 0)),
                  pl.BlockSpec((d, tn), lambda j: (0, j)),
                  pl.BlockSpec((1, tn), lambda j: (0, j))],
        out_specs=pl.BlockSpec((r, tn), lambda j: (0, j)),
        out_shape=jax.ShapeDtypeStruct((r, n), F32),
        compiler_params=_cparams(1, 32),
        name="ada_mod",
    )(c_all, w_ada, b_ada.reshape(1, n))


def _rms(x, g):
    return x * lax.rsqrt(jnp.mean(x * x, axis=-1, keepdims=True) + RMS_EPS) * g


def _normmod_kernel(x_ref, g_ref, sc_ref, sh_ref, o_ref):
    xn = _rms(x_ref[...], g_ref[...])
    o_ref[...] = (xn * (1.0 + sc_ref[...]) + sh_ref[...]).astype(o_ref.dtype)


def _mod_spec(mod, tm, d):
    if mod.shape[0] == 1:
        return pl.BlockSpec((1, d), lambda i: (0, 0))
    return pl.BlockSpec((tm, d), lambda i: (i, 0))


def _normmod(x, g, scale, shift, tm):
    n, d = x.shape
    return pl.pallas_call(
        _normmod_kernel,
        grid=(n // tm,),
        in_specs=[pl.BlockSpec((tm, d), lambda i: (i, 0)),
                  pl.BlockSpec((1, d), lambda i: (0, 0)),
                  _mod_spec(scale, tm, d), _mod_spec(shift, tm, d)],
        out_specs=pl.BlockSpec((tm, d), lambda i: (i, 0)),
        out_shape=jax.ShapeDtypeStruct((n, d), BF16),
        compiler_params=_cparams(1, 32),
        name="norm_mod",
    )(x, g.reshape(1, d), scale, shift)


def _mm_kernel(a_ref, w_ref, *refs, epilogue, n_out):
    out_refs, wb_ref = refs[:n_out], refs[n_out]

    @pl.when(pl.program_id(1) == 0)
    def _():
        wb_ref[...] = w_ref[...].astype(BF16)

    acc = jnp.dot(a_ref[...], wb_ref[...], preferred_element_type=F32)
    for r, v in zip(out_refs, epilogue(acc)):
        r[...] = v.astype(r.dtype)


def _mm(a, w, col0, ncols, out_dtypes, epilogue, tm, tn, name):
    m, k = a.shape
    cb0 = col0 // tn
    out_spec = pl.BlockSpec((tm, tn), lambda j, i: (i, j))
    return pl.pallas_call(
        functools.partial(_mm_kernel, epilogue=epilogue, n_out=len(out_dtypes)),
        grid=(ncols // tn, m // tm),
        in_specs=[pl.BlockSpec((tm, k), lambda j, i: (i, 0)),
                  pl.BlockSpec((k, tn), lambda j, i: (0, cb0 + j))],
        out_specs=[out_spec] * len(out_dtypes),
        out_shape=[jax.ShapeDtypeStruct((m, ncols), dt) for dt in out_dtypes],
        scratch_shapes=[pltpu.VMEM((k, tn), BF16)],
        compiler_params=_cparams(2, 48),
        name=name,
    )(a, w)


def _pool_groups(win_fn, u_fn, pos, mix_ref, scale_ref, store_fn, gw):
    for g, w in enumerate(POOL_WINDOWS):
        lo = g * gw
        win = win_fn(0, lo)
        for dlt in range(1, w):
            win = win + win_fn(dlt, lo)
        cnt = jnp.minimum(pos + 1, w).astype(F32)
        dd = win / cnt - u_fn(lo)
        mixed = jnp.dot(dd.astype(BF16), mix_ref[g].astype(BF16), preferred_element_type=F32)
        store_fn(lo, mixed * scale_ref[:, lo:lo + gw])


def _pool_prompt_kernel(hist_ref, u_ref, mix_ref, scale_ref, o_ref, ext_ref, *, tm, gw):
    i = pl.program_id(0)
    ext_ref[0:POOL_HIST, :] = jnp.where(i == 0, 0.0, hist_ref[...])
    ext_ref[POOL_HIST:POOL_HIST + tm, :] = u_ref[...]
    pos = i * tm + lax.broadcasted_iota(jnp.int32, (tm, 1), 0)

    def store(lo, val):
        o_ref[:, lo:lo + gw] = val.astype(o_ref.dtype)

    _pool_groups(lambda dlt, lo: ext_ref[pl.ds(POOL_HIST - dlt, tm), lo:lo + gw],
                 lambda lo: u_ref[:, lo:lo + gw], pos, mix_ref, scale_ref, store, gw)


def _pool_prompt(u, pool_mix, pool_scale, tm=512):
    n, pw = u.shape
    ng, gw, _ = pool_mix.shape
    hb = tm // POOL_HIST
    return pl.pallas_call(
        functools.partial(_pool_prompt_kernel, tm=tm, gw=gw),
        grid=(n // tm,),
        in_specs=[pl.BlockSpec((POOL_HIST, pw), lambda i: (jnp.maximum(i * hb - 1, 0), 0)),
                  pl.BlockSpec((tm, pw), lambda i: (i, 0)),
                  pl.BlockSpec((ng, gw, gw), lambda i: (0, 0, 0)),
                  pl.BlockSpec((1, pw), lambda i: (0, 0))],
        out_specs=pl.BlockSpec((tm, pw), lambda i: (i, 0)),
        out_shape=jax.ShapeDtypeStruct((n, pw), BF16),
        scratch_shapes=[pltpu.VMEM((POOL_HIST + tm, pw), F32)],
        compiler_params=_cparams(1, 32),
        name="pool_prompt",
    )(u, u, pool_mix, pool_scale.reshape(1, pw))


def _pool_sample_kernel(hist_ref, u_ref, mix_ref, scale_ref, o_ref, ext_ref, *, nb, t, gw, past_len):
    ext_ref[:, 0:POOL_HIST, :] = hist_ref[...]
    ext_ref[:, POOL_HIST:POOL_HIST + t, :] = u_ref[...]
    pos = past_len + (lax.broadcasted_iota(jnp.int32, (nb * t, 1), 0) & (t - 1))

    def store(lo, val):
        o_ref[:, lo:lo + gw] = val.astype(o_ref.dtype)

    _pool_groups(lambda dlt, lo: ext_ref[:, pl.ds(POOL_HIST - dlt, t), lo:lo + gw].reshape(nb * t, gw),
                 lambda lo: u_ref[:, :, lo:lo + gw].reshape(nb * t, gw), pos, mix_ref, scale_ref,
                 store, gw)


def _pool_sample(u3, hist3, pool_mix, pool_scale, past_len):
    nb, t, pw = u3.shape
    ng, gw, _ = pool_mix.shape
    return pl.pallas_call(
        functools.partial(_pool_sample_kernel, nb=nb, t=t, gw=gw, past_len=past_len),
        grid=(1,),
        in_specs=[pl.BlockSpec((nb, POOL_HIST, pw), lambda i: (0, 0, 0)),
                  pl.BlockSpec((nb, t, pw), lambda i: (0, 0, 0)),
                  pl.BlockSpec((ng, gw, gw), lambda i: (0, 0, 0)),
                  pl.BlockSpec((1, pw), lambda i: (0, 0))],
        out_specs=pl.BlockSpec((nb * t, pw), lambda i: (0, 0)),
        out_shape=jax.ShapeDtypeStruct((nb * t, pw), BF16),
        scratch_shapes=[pltpu.VMEM((nb, POOL_HIST + t, pw), F32)],
        compiler_params=_cparams(1, 32),
        name="pool_sample",
    )(hist3, u3, pool_mix, pool_scale.reshape(1, pw))


def _softplus2(y):
    return jnp.maximum(y, 0.0) + jnp.log2(1.0 + jnp.exp2(-jnp.abs(y)))


def _suffix_matrix(nk, nrep):
    r = lax.broadcasted_iota(jnp.int32, (nk, nk + nrep), 0)
    c = lax.broadcasted_iota(jnp.int32, (nk, nk + nrep), 1)
    return jnp.where((r > c) | (c >= nk), 1.0, 0.0).astype(BF16)


def _attn_prompt_kernel(bias_ref, q_ref, k_ref, v_ref, o_ref, acc_ref, car_ref, ls_ref, r_ref,
                        *, tq, tk):
    h = pl.program_id(0)
    i = pl.program_id(1)
    bias = bias_ref[h]
    q = q_ref[...]
    nsub = tq // tk
    sufm = _suffix_matrix(tk, HEAD_DIM)
    row = lax.broadcasted_iota(jnp.int32, (tq, tk), 0)
    col = lax.broadcasted_iota(jnp.int32, (tq, tk), 1)

    def sub_blocks(j):
        for u in range(nsub):
            off = (nsub - 1 - u) * tk
            yield u, off, pl.multiple_of(j * tq + off, tk)

    def scores(j, masked, slot):
        for u, off, start in sub_blocks(j):
            y = lax.dot_general(q, k_ref[pl.ds(start, tk), :], (((1,), (1,)), ((), ())),
                                preferred_element_type=F32) + bias
            s2 = _softplus2(y)
            lk = jnp.where(col + off < row, s2, 0.0) if masked else s2
            r_ref[slot, u] = jnp.dot(lk.astype(BF16), sufm, preferred_element_type=F32)
            ls_ref[slot, u] = y - s2

    def weights(j, masked, slot):
        car = car_ref[...]
        acc = acc_ref[...]
        for u, off, start in sub_blocks(j):
            a = jnp.exp2(ls_ref[slot, u] - r_ref[slot, u, :, 0:tk]
                         - jnp.concatenate([car] * (tk // HEAD_DIM), axis=1))
            if masked:
                a = jnp.where(col + off < row, a, 0.0)
            acc = acc + jnp.dot(a.astype(BF16), v_ref[pl.ds(start, tk), :],
                                preferred_element_type=F32)
            car = car + r_ref[slot, u, :, tk:tk + HEAD_DIM]
        acc_ref[...] = acc
        car_ref[...] = car

    acc_ref[...] = jnp.zeros_like(acc_ref)
    car_ref[...] = jnp.zeros_like(car_ref)
    scores(i, True, 0)

    @pl.when(i == 0)
    def _():
        weights(0, True, 0)

    @pl.when(i > 0)
    def _():
        weights(i, True, 0)
        scores(i - 1, False, 1)

        def body(s, c):
            j = i - 1 - s
            slot = (s + 1) & 1
            weights(j, False, slot)
            scores(j - 1, False, 1 - slot)
            return c

        lax.fori_loop(0, i - 1, body, 0)
        weights(0, False, i & 1)

    o_ref[...] = acc_ref[...].astype(o_ref.dtype)


def _attn_prompt(qs, kb, vb, bias2, tq=ATTN_TQ, tk=ATTN_TK):
    s, aw = qs.shape
    nh = aw // HEAD_DIM
    nsub = tq // tk
    return pl.pallas_call(
        functools.partial(_attn_prompt_kernel, tq=tq, tk=tk),
        grid_spec=pltpu.PrefetchScalarGridSpec(
            num_scalar_prefetch=1,
            grid=(nh, s // tq),
            in_specs=[pl.BlockSpec((tq, HEAD_DIM), lambda h, i, b: (i, h)),
                      pl.BlockSpec((s, HEAD_DIM), lambda h, i, b: (0, h)),
                      pl.BlockSpec((s, HEAD_DIM), lambda h, i, b: (0, h))],
            out_specs=pl.BlockSpec((tq, HEAD_DIM), lambda h, i, b: (i, h)),
            scratch_shapes=[pltpu.VMEM((tq, HEAD_DIM), F32), pltpu.VMEM((tq, HEAD_DIM), F32),
                            pltpu.VMEM((2, nsub, tq, tk), F32),
                            pltpu.VMEM((2, nsub, tq, tk + HEAD_DIM), F32)]),
        out_shape=jax.ShapeDtypeStruct((s, aw), BF16),
        compiler_params=_cparams(2, 32),
        name="attn_prompt",
    )(bias2, qs, kb, vb)


def _attn_sample_kernel(pt_ref, bias_ref, q_ref, kn_ref, vn_ref, ck_ref, cv_ref, o_ref,
                        acc_ref, car_ref, kpad_ref, vpad_ref, kbuf, vbuf, sem,
                        *, pps, nh, t, skew, n_pages):
    b = pl.program_id(0)
    c = pl.program_id(1)
    nc = pl.num_programs(1)
    step = b * nc + c
    slot = step & 1
    ht = nh * t
    page = HEAD_DIM
    sufm = _suffix_matrix(page, HEAD_DIM)
    q = q_ref[...]
    head = lambda h: slice(h * HEAD_DIM, (h + 1) * HEAD_DIM)
    rows = lambda h: slice(h * t, (h + 1) * t)
    qall = jnp.concatenate([q[:, head(h)] for h in range(nh)], axis=0).astype(BF16)
    bias_rows = jnp.concatenate([jnp.full((t, page), bias_ref[h], F32) for h in range(nh)], axis=0)
    kpos = lax.broadcasted_iota(jnp.int32, (ht, page), 1)
    qpos = lax.broadcasted_iota(jnp.int32, (ht, page), 0) & (t - 1)

    def page_copies(bb, cc, sl):
        cps = []
        for i in range(pps):
            pg = pt_ref[bb, n_pages - 1 - (cc * pps + i)]
            for h in range(nh):
                cps.append(pltpu.make_async_copy(ck_ref.at[pg, :, h, :], kbuf.at[sl, i, h], sem.at[sl]))
                cps.append(pltpu.make_async_copy(cv_ref.at[pg, :, h, :], vbuf.at[sl, i, h], sem.at[sl]))
        return cps

    def scores(k3, masked):
        k2 = k3.reshape(nh * page, HEAD_DIM).astype(BF16)
        z = lax.dot_general(qall, k2, (((1,), (1,)), ((), ())), preferred_element_type=F32)
        y = jnp.concatenate([z[rows(h), h * page:(h + 1) * page] for h in range(nh)], axis=0) + bias_rows
        s2 = _softplus2(y)
        lk = jnp.where(kpos < qpos, s2, 0.0) if masked else s2
        return y - s2, jnp.dot(lk.astype(BF16), sufm, preferred_element_type=F32)

    def weights(v3, ls, r, car, acc, masked):
        a = jnp.exp2(ls - r[:, :page] - car)
        if masked:
            a = jnp.where(kpos < qpos, a, 0.0)
        zero = jnp.zeros((t, page), F32)
        a_bd = jnp.concatenate(
            [jnp.concatenate([a[rows(h)] if g == h else zero for g in range(nh)], axis=1)
             for h in range(nh)], axis=0).astype(BF16)
        v2 = v3.reshape(nh * page, HEAD_DIM).astype(BF16)
        return car + r[:, page:], acc + jnp.dot(a_bd, v2, preferred_element_type=F32)

    def process(pages, masked):
        car = car_ref[...]
        acc = acc_ref[...]
        staged = []
        for idx in range(len(pages) + skew):
            if idx < len(pages):
                staged.append(scores(pages[idx][0](), masked))
            if idx >= skew:
                car, acc = weights(pages[idx - skew][1](), *staged[idx - skew], car, acc, masked)
        car_ref[...] = car
        acc_ref[...] = acc

    @pl.when(step == 0)
    def _():
        for cp in page_copies(0, 0, 0):
            cp.start()

    for cp in page_copies(b, c, slot):
        cp.wait()

    @pl.when(step + 1 < pl.num_programs(0) * nc)
    def _():
        wrap = c + 1 == nc
        for cp in page_copies(jnp.where(wrap, b + 1, b), jnp.where(wrap, 0, c + 1), 1 - slot):
            cp.start()

    @pl.when(c == 0)
    def _():
        acc_ref[...] = jnp.zeros_like(acc_ref)
        car_ref[...] = jnp.zeros_like(car_ref)
        kpad_ref[...] = jnp.zeros_like(kpad_ref)
        vpad_ref[...] = jnp.zeros_like(vpad_ref)
        for h in range(nh):
            kpad_ref[h, 0:t, :] = kn_ref[:, head(h)]
            vpad_ref[h, 0:t, :] = vn_ref[:, head(h)]
        process([(lambda: kpad_ref[...], lambda: vpad_ref[...])], True)

    process([(functools.partial(lambda i: kbuf[slot, i], i), functools.partial(lambda i: vbuf[slot, i], i))
             for i in range(pps)], False)

    @pl.when(c == nc - 1)
    def _():
        acc = acc_ref[...]
        for h in range(nh):
            o_ref[:, head(h)] = acc[rows(h)]


def _attn_sample(q, kn, vn, cache_k, cache_v, page_table, bias2, t, pps=PAGES_PER_STEP):
    nbt, aw = q.shape
    nb = nbt // t
    nh = aw // HEAD_DIM
    n_pages = page_table.shape[1]
    page = cache_k.shape[1]
    assert cache_k.shape[1:] == (HEAD_DIM, nh, HEAD_DIM) and n_pages % pps == 0
    row_spec = pl.BlockSpec((t, aw), lambda b, c, pt, bias: (b, 0))
    any_spec = pl.BlockSpec(memory_space=pl.ANY)
    return pl.pallas_call(
        functools.partial(_attn_sample_kernel, pps=pps, nh=nh, t=t, skew=PAGE_SKEW, n_pages=n_pages),
        grid_spec=pltpu.PrefetchScalarGridSpec(
            num_scalar_prefetch=2,
            grid=(nb, n_pages // pps),
            in_specs=[row_spec, row_spec, row_spec, any_spec, any_spec],
            out_specs=row_spec,
            scratch_shapes=[pltpu.VMEM((nh * t, HEAD_DIM), F32), pltpu.VMEM((nh * t, HEAD_DIM), F32),
                            pltpu.VMEM((nh, page, HEAD_DIM), F32),
                            pltpu.VMEM((nh, page, HEAD_DIM), F32),
                            pltpu.VMEM((2, pps, nh, page, HEAD_DIM), F32),
                            pltpu.VMEM((2, pps, nh, page, HEAD_DIM), F32),
                            pltpu.SemaphoreType.DMA((2,))]),
        out_shape=jax.ShapeDtypeStruct((nbt, aw), F32),
        compiler_params=_cparams(2, 40),
        name="attn_sample",
    )(page_table, bias2, q, kn, vn, cache_k, cache_v)


def _merge_kernel(po_ref, at_ref, gt_ref, x_ref, g1_ref, n2_ref, sc_ref, sh_ref,
                  wbp_ref, wba_ref, wo_ref, *refs, d, n_blocks):
    x1_ref, h2_ref = refs[-2:]
    i = pl.program_id(0)

    @pl.when(i < n_blocks)
    def _():
        bp = jnp.dot(po_ref[...], wbp_ref[...], preferred_element_type=F32)
        ba = jnp.dot(at_ref[...].astype(BF16), wba_ref[...], preferred_element_type=F32)
        merged = gt_ref[:, :d] * bp + gt_ref[:, d:] * ba
        x1 = x_ref[...] + g1_ref[...] * jnp.dot(merged.astype(BF16), wo_ref[...],
                                                preferred_element_type=F32)
        x1_ref[...] = x1
        h2_ref[...] = _rms(x1, n2_ref[...]) * (1.0 + sc_ref[...]) + sh_ref[...]

    @pl.when(i >= n_blocks)
    def _():
        h2_ref[...] = jnp.zeros_like(h2_ref)


def _merge(pool_out, attn, gates, x, gate1, norm2_g, scale2, shift2, wbp, wba, wo, tm,
           n_all, row0, h2_prev=None):
    n, d = x.shape
    pw, aw = pool_out.shape[1], attn.shape[1]
    rb0 = row0 // tm
    n_blocks = n // tm
    steps = n_blocks if h2_prev is not None else (n_all - row0) // tm
    blk = lambda i: jnp.minimum(i, n_blocks - 1)
    row = lambda w: pl.BlockSpec((tm, w), lambda i: (blk(i), 0))
    mod = lambda m: (pl.BlockSpec((1, d), lambda i: (0, 0)) if m.shape[0] == 1 else row(d))
    in_specs = [row(pw), row(aw), row(2 * d), row(d), mod(gate1),
                pl.BlockSpec((1, d), lambda i: (0, 0)), mod(scale2), mod(shift2),
                _resident(wbp.shape), _resident(wba.shape), _resident(wo.shape)]
    args = [pool_out, attn, gates, x, gate1, norm2_g.reshape(1, d), scale2, shift2, wbp, wba, wo]
    aliases = {}
    if h2_prev is not None:
        in_specs.append(pl.BlockSpec(memory_space=pl.ANY))
        args.append(h2_prev)
        aliases = {len(args) - 1: 1}
    return pl.pallas_call(
        functools.partial(_merge_kernel, d=d, n_blocks=n_blocks),
        grid=(steps,),
        in_specs=in_specs,
        out_specs=[row(d), pl.BlockSpec((tm, d), lambda i: (rb0 + i, 0))],
        out_shape=[jax.ShapeDtypeStruct((n, d), F32), jax.ShapeDtypeStruct((n_all, d), F32)],
        input_output_aliases=aliases,
        compiler_params=_cparams(1, 56),
        name="merge",
    )(*args)


def _first_max(v, ids, n):
    m = jnp.max(v, axis=0, keepdims=True)
    idx = jnp.min(jnp.where(v == m, ids, n), axis=0, keepdims=True)
    return m, idx


def _router_kernel(h_ref, wr_ref, rb_ref, ei_ref, wt_ref, *, ne, ng, tkg, topk):
    epg = ne // ng
    hb = h_ref[...].astype(BF16)
    logits = lax.dot_general(wr_ref[...], hb, (((1,), (1,)), ((), ())), preferred_element_type=F32)
    scores = jax.nn.sigmoid(logits)
    sel = scores + rb_ref[...]
    tm = sel.shape[1]
    neg = -jnp.inf
    fiota = lambda n: lax.broadcasted_iota(jnp.int32, (n, tm), 0).astype(F32)
    ids_g = fiota(epg)
    gs = []
    for g in range(ng):
        v = sel[g * epg:(g + 1) * epg, :]
        m1, i1 = _first_max(v, ids_g, float(epg))
        m2 = jnp.max(jnp.where(ids_g == i1, neg, v), axis=0, keepdims=True)
        gs.append(m1 + m2)
    gsc = jnp.concatenate(gs, axis=0)
    ids_n = fiota(ng)
    gmask = jnp.zeros((ng, tm), F32)
    for _ in range(tkg):
        _, gi = _first_max(gsc, ids_n, float(ng))
        hit = ids_n == gi
        gmask = jnp.where(hit, 1.0, gmask)
        gsc = jnp.where(hit, neg, gsc)
    emask = jnp.concatenate(
        [jnp.broadcast_to(gmask[g:g + 1, :], (epg, tm)) for g in range(ng)], axis=0)
    cand = jnp.where(emask > 0.5, sel, neg)
    ids_e = fiota(ne)
    eis, wts = [], []
    for _ in range(topk):
        _, ei = _first_max(cand, ids_e, float(ne))
        hit = ids_e == ei
        eis.append(ei)
        wts.append(jnp.sum(jnp.where(hit, scores, 0.0), axis=0, keepdims=True))
        cand = jnp.where(hit, neg, cand)
    w = jnp.concatenate(wts, axis=0)
    ei_ref[...] = jnp.concatenate(eis, axis=0).astype(jnp.int32)
    wt_ref[...] = w / jnp.sum(w, axis=0, keepdims=True) * ROUTED_SCALE


def _router(h2, wr_t, router_bias, tm):
    n, d = h2.shape
    ne = wr_t.shape[0]
    return pl.pallas_call(
        functools.partial(_router_kernel, ne=ne, ng=N_EXPERT_GROUPS, tkg=TOPK_GROUPS, topk=TOP_K),
        grid=(n // tm,),
        in_specs=[pl.BlockSpec((tm, d), lambda i: (i, 0)),
                  pl.BlockSpec((ne, d), lambda i: (0, 0)),
                  pl.BlockSpec((ne, 1), lambda i: (0, 0))],
        out_specs=[pl.BlockSpec((TOP_K, tm), lambda i: (0, i)),
                   pl.BlockSpec((TOP_K, tm), lambda i: (0, i))],
        out_shape=[jax.ShapeDtypeStruct((TOP_K, n), jnp.int32),
                   jax.ShapeDtypeStruct((TOP_K, n), F32)],
        compiler_params=_cparams(1, 32),
        name="router",
    )(h2, wr_t, router_bias.reshape(ne, 1))


def _expert_kernel(be_ref, nu_ref, tok_ref, tokn_ref, dstp_ref, dst_ref, h_ref, wg_ref, wu_ref,
                   wd_ref, y_ref, xbuf, ybuf, gsem, ssem, *, tm, unroll, pad_row0, n_chunk):
    b = pl.program_id(0)
    nu = nu_ref[0]
    slot = b & 1
    other = 1 - slot
    ed = wg_ref.shape[1]

    def gather(tref, s, r):
        return pltpu.make_async_copy(h_ref.at[pl.ds(tref[0, 0, r], 1)], xbuf.at[s, pl.ds(r, 1)],
                                     gsem.at[s])

    def scatter(row, s, r):
        return pltpu.make_async_copy(ybuf.at[s, pl.ds(r, 1)], y_ref.at[pl.ds(row, 1)], ssem.at[s])

    def wait_gather(s):
        pltpu.make_async_copy(h_ref.at[pl.ds(0, tm)], xbuf.at[s], gsem.at[s]).wait()

    def wait_scatter(s):
        pltpu.make_async_copy(ybuf.at[s], y_ref.at[pl.ds(0, tm)], ssem.at[s]).wait()

    def rolled(start_row):
        def body(i, c):
            for u in range(unroll):
                start_row(i * unroll + u)
            return c

        lax.fori_loop(0, tm // unroll, body, 0)

    @pl.when(b < nu)
    def _():
        @pl.when(b == 0)
        def _():
            ybuf[...] = jnp.zeros_like(ybuf)
            rolled(lambda r: gather(tok_ref, 0, r).start())
            rolled(lambda r: scatter(pad_row0 + r, 0, r).start())

        wait_gather(slot)
        starts = ([functools.partial(lambda r: gather(tokn_ref, other, r).start(), r) for r in range(tm)]
                  + [functools.partial(lambda r: scatter(dstp_ref[0, 0, r], other, r).start(), r)
                     for r in range(tm)])
        n_groups = 3 * n_chunk
        per = -(-len(starts) // n_groups)

        def issue_group(g):
            for st in starts[g * per:(g + 1) * per]:
                st()

        x = xbuf[slot].astype(BF16)
        cw = ed // n_chunk
        acts = []
        for c in range(n_chunk):
            cs = slice(c * cw, (c + 1) * cw)
            gate = jnp.dot(x, wg_ref[:, cs].astype(BF16), preferred_element_type=F32)
            issue_group(2 * c)
            up = jnp.dot(x, wu_ref[:, cs].astype(BF16), preferred_element_type=F32)
            issue_group(2 * c + 1)
            acts.append((_silu(gate) * up).astype(BF16))
        y = None
        for c in range(n_chunk):
            part = jnp.dot(acts[c], wd_ref[c * cw:(c + 1) * cw, :].astype(BF16),
                           preferred_element_type=F32)
            y = part if y is None else y + part
            issue_group(2 * n_chunk + c)
        wait_scatter(slot)
        ybuf[slot] = y

        @pl.when(b == nu - 1)
        def _():
            wait_gather(other)
            rolled(lambda r: scatter(dst_ref[0, 0, r], slot, r).start())
            wait_scatter(other)
            wait_scatter(slot)


def _experts(block_e, n_used, slot_tok, slot_dst, h2, w_gate, w_up, w_down, tm=MOE_TM):
    nb = block_e.shape[0]
    n_all, d = h2.shape
    ed = w_gate.shape[2]
    n_rows = n_all * TOP_K
    cur = pl.BlockSpec((1, 1, tm), lambda b, be, nu: (b, 0, 0), memory_space=pltpu.SMEM)
    nxt = pl.BlockSpec((1, 1, tm), lambda b, be, nu: (jnp.minimum(b + 1, nb - 1), 0, 0),
                       memory_space=pltpu.SMEM)
    any_spec = pl.BlockSpec(memory_space=pl.ANY)
    tok3 = slot_tok.reshape(nb, 1, tm)
    dst3 = slot_dst.reshape(nb, 1, tm)
    pad1 = (n_rows + tm + jnp.arange(tm, dtype=jnp.int32)).reshape(1, 1, tm)
    dstp3 = jnp.concatenate([pad1, dst3[:-1]], axis=0)
    return pl.pallas_call(
        functools.partial(_expert_kernel, tm=tm, unroll=MOE_DMA_UNROLL, pad_row0=n_rows,
                          n_chunk=MOE_N_CHUNK),
        grid_spec=pltpu.PrefetchScalarGridSpec(
            num_scalar_prefetch=2,
            grid=(nb,),
            in_specs=[cur, nxt, cur, cur, any_spec,
                      pl.BlockSpec((None, d, ed), lambda b, be, nu: (be[b], 0, 0)),
                      pl.BlockSpec((None, d, ed), lambda b, be, nu: (be[b], 0, 0)),
                      pl.BlockSpec((None, ed, d), lambda b, be, nu: (be[b], 0, 0))],
            out_specs=any_spec,
            scratch_shapes=[pltpu.VMEM((2, tm, d), F32), pltpu.VMEM((2, tm, d), F32),
                            pltpu.SemaphoreType.DMA((2,)), pltpu.SemaphoreType.DMA((2,))]),
        out_shape=jax.ShapeDtypeStruct((n_rows + 2 * tm, d), F32),
        compiler_params=_cparams(1, 56),
        name="experts",
    )(block_e, n_used, tok3, tok3, dstp3, dst3, h2, w_gate, w_up, w_down)


def _dispatch_meta(eidx, n_experts, tm):
    n = eidx.shape[0]
    nk = n * TOP_K
    nb = -(-nk // tm) + n_experts
    flat_e = eidx.reshape(-1)
    order = jnp.argsort(flat_e, stable=True).astype(jnp.int32)
    counts = jnp.sum((flat_e[:, None] == jnp.arange(n_experts, dtype=jnp.int32)[None, :])
                     .astype(jnp.int32), axis=0)
    starts = jnp.cumsum(counts) - counts
    padded = (counts + tm - 1) // tm * tm
    pends = jnp.cumsum(padded)
    pstarts = pends - padded
    n_used = (pends[-1] // tm).astype(jnp.int32)
    blk = jnp.arange(nb, dtype=jnp.int32)
    blk_start = blk * tm
    be = jnp.minimum(jnp.sum((pends[None, :] <= blk_start[:, None]).astype(jnp.int32), axis=1),
                     n_experts - 1)
    be = jnp.where(blk < n_used, be, be[jnp.maximum(n_used - 1, 0)]).astype(jnp.int32)
    within = (blk_start - pstarts[be])[:, None] + jnp.arange(tm, dtype=jnp.int32)[None, :]
    valid = within < counts[be][:, None]
    flat = order[jnp.clip(starts[be][:, None] + within, 0, nk - 1)]
    tok = flat // TOP_K
    pad_row = nk + (blk & 1)[:, None] * tm + jnp.arange(tm, dtype=jnp.int32)[None, :]
    dst = jnp.where(valid, (flat % TOP_K) * n + tok, pad_row)
    return be, n_used.reshape(1), tok.astype(jnp.int32), dst.astype(jnp.int32)


def _combine_kernel(*refs, d, sd):
    y_refs = refs[:TOP_K]
    wt_ref, h_ref, x_ref, g2_ref, fg_ref, wgu_ref, wd_ref, o_ref = refs[TOP_K:]
    wt = wt_ref[...]
    routed = y_refs[0][...] * wt[:, 0:1]
    for k in range(1, TOP_K):
        routed = routed + y_refs[k][...] * wt[:, k:k + 1]
    hb = h_ref[...].astype(BF16)
    gu = jnp.dot(hb, wgu_ref[...], preferred_element_type=F32)
    act = (_silu(gu[:, :sd]) * gu[:, sd:]).astype(BF16)
    ffn = routed + jnp.dot(act, wd_ref[...], preferred_element_type=F32)
    x2 = x_ref[...] + g2_ref[...] * ffn
    o_ref[...] = _rms(x2, fg_ref[...])


def _combine(y, n_all, row0, wts, h2, x1, gate2, final_g, ws_gu, ws_d, tm):
    n, d = x1.shape
    sd = ws_d.shape[0]
    rb0 = row0 // tm
    kb = n_all // tm
    row = lambda w: pl.BlockSpec((tm, w), lambda i: (i, 0))
    shifted = lambda w: pl.BlockSpec((tm, w), lambda i: (rb0 + i, 0))
    y_specs = [pl.BlockSpec((tm, d), functools.partial(lambda i, k: (k * kb + rb0 + i, 0), k=k))
               for k in range(TOP_K)]
    return pl.pallas_call(
        functools.partial(_combine_kernel, d=d, sd=sd),
        grid=(n // tm,),
        in_specs=y_specs + [shifted(TOP_K), shifted(d), row(d), _mod_spec(gate2, tm, d),
                            pl.BlockSpec((1, d), lambda i: (0, 0)),
                            _resident(ws_gu.shape), _resident(ws_d.shape)],
        out_specs=row(d),
        out_shape=jax.ShapeDtypeStruct((n, d), F32),
        compiler_params=_cparams(1, 48),
        name="combine",
    )(*([y] * TOP_K), wts, h2, x1, gate2, final_g.reshape(1, d), ws_gu, ws_d)


def _in_proj(h, w_in, pw, aw, d, tm, q_dtype, want_bf16_kv):
    tn = 512
    ident = lambda acc: (acc,)
    (u,) = _mm(h, w_in, 0, pw, [F32], ident, tm, tn, "proj_u")
    qscale = LOG2E / math.sqrt(HEAD_DIM)
    (q,) = _mm(h, w_in, pw, aw, [q_dtype], lambda acc: (acc * qscale,), tm, tn, "proj_q")
    kv_dt = [F32, BF16] if want_bf16_kv else [F32]
    both = (lambda acc: (acc, acc)) if want_bf16_kv else ident
    k = _mm(h, w_in, pw + aw, aw, kv_dt, both, tm, tn, "proj_k")
    v = _mm(h, w_in, pw + 2 * aw, aw, kv_dt, both, tm, tn, "proj_v")
    (gates,) = _mm(h, w_in, pw + 3 * aw, 2 * d, [F32], lambda acc: (jax.nn.sigmoid(acc),),
                   tm, tn, "proj_gates")
    return u, q, k, v, gates


def kernel(x_prompt, x_sample, c_prompt, c_sample, state_pool, cache_k, cache_v, page_table,
           norm1_g, norm2_g, w_ada, b_ada, w_in, pool_mix, pool_scale, w_br_pool, w_br_attn,
           sb_bias, w_o, w_router, router_bias, w_e_gate, w_e_up, w_e_down, w_s_gate, w_s_up,
           w_s_down, final_g):
    depth = w_in.shape[0]
    nbp, seq, d = x_prompt.shape
    nbs, t, _ = x_sample.shape
    assert nbp == 1 and depth == 1 and t & (t - 1) == 0
    l = 0
    pw = pool_scale.shape[1]
    aw = w_br_attn.shape[1]
    nh = aw // HEAD_DIM
    n_experts = w_router.shape[2]
    n_pages, page = page_table.shape[1], cache_k.shape[2]
    past_len = n_pages * page
    ns = nbs * t
    n_all = seq + ns
    buf = POOL_HIST - 1

    xp = x_prompt.reshape(seq, d)
    xs = x_sample.reshape(ns, d)
    rows = nbp + nbs
    rpad = -(-rows // 8) * 8
    c_all = jnp.concatenate([c_prompt, c_sample, jnp.zeros((rpad - rows, d), F32)], axis=0)

    mod = _ada(c_all, w_ada[l], b_ada[l])
    mod_p = [mod[0:1, i * d:(i + 1) * d] for i in range(N_MOD)]
    mod_s = [jnp.repeat(mod[nbp:rows, i * d:(i + 1) * d], t, axis=0) for i in range(N_MOD)]
    wbp, wba, wo = (w.astype(BF16) for w in (w_br_pool[l], w_br_attn[l], w_o[l]))
    wr_t = w_router[l].T.astype(BF16)
    ws_gu = jnp.concatenate([w_s_gate[l], w_s_up[l]], axis=1).astype(BF16)
    ws_d = w_s_down[l].astype(BF16)
    bias2 = sb_bias[l] * LOG2E

    hp = _normmod(xp, norm1_g[l], mod_p[1], mod_p[0], 512)
    u_p, q_p, (k_p, kb_p), (v_p, vb_p), gates_p = _in_proj(hp, w_in[l], pw, aw, d, 1024, BF16, True)
    pool_p = _pool_prompt(u_p, pool_mix[l], pool_scale[l])
    attn_p = _attn_prompt(q_p, kb_p, vb_p, bias2)
    x1_p, h2 = _merge(pool_p, attn_p, gates_p, xp, mod_p[2], norm2_g[l], mod_p[4], mod_p[3],
                      wbp, wba, wo, 256, n_all, 0)

    hs = _normmod(xs, norm1_g[l], mod_s[1], mod_s[0], ns)
    u_s, q_s, (k_s,), (v_s,), gates_s = _in_proj(hs, w_in[l], pw, aw, d, ns, F32, False)
    hist_s = jnp.concatenate([jnp.zeros((nbs, 1, pw), F32), state_pool[l]], axis=1)
    pool_s = _pool_sample(u_s.reshape(nbs, t, pw), hist_s, pool_mix[l], pool_scale[l], past_len)
    attn_s = _attn_sample(q_s, k_s, v_s, cache_k[l], cache_v[l], page_table, bias2, t)
    x1_s, h2 = _merge(pool_s, attn_s, gates_s, xs, mod_s[2], norm2_g[l], mod_s[4], mod_s[3],
                      wbp, wba, wo, ns, n_all, seq, h2_prev=h2)

    ei, wt = _router(h2, wr_t, router_bias[l], 768)
    be, n_used, slot_tok, slot_dst = _dispatch_meta(ei.T, n_experts, MOE_TM)
    y = _experts(be, n_used, slot_tok, slot_dst, h2, w_e_gate[l], w_e_up[l], w_e_down[l])
    wts = wt.T
    yp = _combine(y, n_all, 0, wts, h2, x1_p, mod_p[5], final_g, ws_gu, ws_d, 128)
    ys = _combine(y, n_all, seq, wts, h2, x1_s, mod_s[5], final_g, ws_gu, ws_d, 128)

    heads = lambda a, b_: a.reshape(1, b_, -1, nh, HEAD_DIM)
    pool_state_s = jnp.concatenate([state_pool[l][:, t:], u_s.reshape(nbs, t, pw)], axis=1)
    return (yp.reshape(nbp, seq, d), ys.reshape(nbs, t, d),
            u_p[seq - buf:].reshape(1, nbp, buf, pw), heads(k_p, nbp), heads(v_p, nbp),
            pool_state_s[None], heads(k_s, nbs), heads(v_s, nbs))
```

```python
import functools
import math

import jax
import jax.numpy as jnp
from jax import lax
from jax.experimental import pallas as pl
from jax.experimental.pallas import tpu as pltpu

F32 = jnp.float32
BF16 = jnp.bfloat16

HEAD_DIM = 128
POOL_WINDOWS = (2, 4, 8, 16)
POOL_HIST = 16
N_EXPERT_GROUPS = 8
TOPK_GROUPS = 4
TOP_K = 8
ROUTED_SCALE = 2.5
RMS_EPS = 1e-6
N_MOD = 6
LOG2E = 1.4426950408889634

MOE_TM = 256
MOE_DMA_UNROLL = 8
MOE_N_CHUNK = 2
ATTN_TQ = 512
ATTN_TK = 256
PAGES_PER_STEP = 8
PAGE_SKEW = 2
MIB = 1024 * 1024


def _cparams(n_axes, vmem_mib):
    return pltpu.CompilerParams(dimension_semantics=("arbitrary",) * n_axes,
                                vmem_limit_bytes=vmem_mib * MIB)


def _silu(x):
    return x * jax.nn.sigmoid(x)


def _resident(shape):
    return pl.BlockSpec(shape, lambda *_: (0,) * len(shape), pipeline_mode=pl.Buffered(1))


def _ada_kernel(c_ref, w_ref, b_ref, o_ref):
    s = _silu(c_ref[...]).astype(BF16)
    o_ref[...] = jnp.dot(s, w_ref[...].astype(BF16), preferred_element_type=F32) + b_ref[...]


def _ada(c_all, w_ada, b_ada, tn=512):
    r, d = c_all.shape
    n = w_ada.shape[1]
    return pl.pallas_call(
        _ada_kernel,
        grid=(n // tn,),
        in_specs=[pl.BlockSpec((r, d), lambda j: (0, 0)),
                  pl.BlockSpec((d, tn), lambda j: (0, j)),
                  pl.BlockSpec((1, tn), lambda j: (0, j))],
        out_specs=pl.BlockSpec((r, tn), lambda j: (0, j)),
        out_shape=jax.ShapeDtypeStruct((r, n), F32),
        compiler_params=_cparams(1, 32),
        name="ada_mod",
    )(c_all, w_ada, b_ada.reshape(1, n))


def _rms(x, g):
    return x * lax.rsqrt(jnp.mean(x * x, axis=-1, keepdims=True) + RMS_EPS) * g


def _normmod_kernel(x_ref, g_ref, sc_ref, sh_ref, o_ref):
    xn = _rms(x_ref[...], g_ref[...])
    o_ref[...] = (xn * (1.0 + sc_ref[...]) + sh_ref[...]).astype(o_ref.dtype)


def _mod_spec(mod, tm, d):
    if mod.shape[0] == 1:
        return pl.BlockSpec((1, d), lambda i: (0, 0))
    return pl.BlockSpec((tm, d), lambda i: (i, 0))


def _normmod(x, g, scale, shift, tm):
    n, d = x.shape
    return pl.pallas_call(
        _normmod_kernel,
        grid=(n // tm,),
        in_specs=[pl.BlockSpec((tm, d), lambda i: (i, 0)),
                  pl.BlockSpec((1, d), lambda i: (0, 0)),
                  _mod_spec(scale, tm, d), _mod_spec(shift, tm, d)],
        out_specs=pl.BlockSpec((tm, d), lambda i: (i, 0)),
        out_shape=jax.ShapeDtypeStruct((n, d), BF16),
        compiler_params=_cparams(1, 32),
        name="norm_mod",
    )(x, g.reshape(1, d), scale, shift)


def _mm_kernel(a_ref, w_ref, *refs, epilogue, n_out):
    out_refs, wb_ref = refs[:n_out], refs[n_out]

    @pl.when(pl.program_id(1) == 0)
    def _():
        wb_ref[...] = w_ref[...].astype(BF16)

    acc = jnp.dot(a_ref[...], wb_ref[...], preferred_element_type=F32)
    for r, v in zip(out_refs, epilogue(acc)):
        r[...] = v.astype(r.dtype)


def _mm(a, w, col0, ncols, out_dtypes, epilogue, tm, tn, name):
    m, k = a.shape
    cb0 = col0 // tn
    out_spec = pl.BlockSpec((tm, tn), lambda j, i: (i, j))
    return pl.pallas_call(
        functools.partial(_mm_kernel, epilogue=epilogue, n_out=len(out_dtypes)),
        grid=(ncols // tn, m // tm),
        in_specs=[pl.BlockSpec((tm, k), lambda j, i: (i, 0)),
                  pl.BlockSpec((k, tn), lambda j, i: (0, cb0 + j))],
        out_specs=[out_spec] * len(out_dtypes),
        out_shape=[jax.ShapeDtypeStruct((m, ncols), dt) for dt in out_dtypes],
        scratch_shapes=[pltpu.VMEM((k, tn), BF16)],
        compiler_params=_cparams(2, 56),
        name=name,
    )(a, w)


def _pool_groups(win_fn, u_fn, pos, mix_ref, scale_ref, store_fn, gw):
    for g, w in enumerate(POOL_WINDOWS):
        lo = g * gw
        win = win_fn(0, lo)
        for dlt in range(1, w):
            win = win + win_fn(dlt, lo)
        cnt = jnp.minimum(pos + 1, w).astype(F32)
        dd = win / cnt - u_fn(lo)
        mixed = jnp.dot(dd.astype(BF16), mix_ref[g].astype(BF16), preferred_element_type=F32)
        store_fn(lo, mixed * scale_ref[:, lo:lo + gw])


def _pool_prompt_kernel(hist_ref, u_ref, mix_ref, scale_ref, o_ref, ext_ref, *, tm, gw):
    i = pl.program_id(0)
    ext_ref[0:POOL_HIST, :] = jnp.where(i == 0, 0.0, hist_ref[...])
    ext_ref[POOL_HIST:POOL_HIST + tm, :] = u_ref[...]
    pos = i * tm + lax.broadcasted_iota(jnp.int32, (tm, 1), 0)

    def store(lo, val):
        o_ref[:, lo:lo + gw] = val.astype(o_ref.dtype)

    _pool_groups(lambda dlt, lo: ext_ref[pl.ds(POOL_HIST - dlt, tm), lo:lo + gw],
                 lambda lo: u_ref[:, lo:lo + gw], pos, mix_ref, scale_ref, store, gw)


def _pool_prompt(u, pool_mix, pool_scale, tm=512):
    n, pw = u.shape
    ng, gw, _ = pool_mix.shape
    hb = tm // POOL_HIST
    return pl.pallas_call(
        functools.partial(_pool_prompt_kernel, tm=tm, gw=gw),
        grid=(n // tm,),
        in_specs=[pl.BlockSpec((POOL_HIST, pw), lambda i: (jnp.maximum(i * hb - 1, 0), 0)),
                  pl.BlockSpec((tm, pw), lambda i: (i, 0)),
                  pl.BlockSpec((ng, gw, gw), lambda i: (0, 0, 0)),
                  pl.BlockSpec((1, pw), lambda i: (0, 0))],
        out_specs=pl.BlockSpec((tm, pw), lambda i: (i, 0)),
        out_shape=jax.ShapeDtypeStruct((n, pw), BF16),
        scratch_shapes=[pltpu.VMEM((POOL_HIST + tm, pw), F32)],
        compiler_params=_cparams(1, 32),
        name="pool_prompt",
    )(u, u, pool_mix, pool_scale.reshape(1, pw))


def _pool_sample_kernel(hist_ref, u_ref, mix_ref, scale_ref, o_ref, ext_ref, *, nb, t, gw, past_len):
    ext_ref[:, 0:POOL_HIST, :] = hist_ref[...]
    ext_ref[:, POOL_HIST:POOL_HIST + t, :] = u_ref[...]
    pos = past_len + (lax.broadcasted_iota(jnp.int32, (nb * t, 1), 0) & (t - 1))

    def store(lo, val):
        o_ref[:, lo:lo + gw] = val.astype(o_ref.dtype)

    _pool_groups(lambda dlt, lo: ext_ref[:, pl.ds(POOL_HIST - dlt, t), lo:lo + gw].reshape(nb * t, gw),
                 lambda lo: u_ref[:, :, lo:lo + gw].reshape(nb * t, gw), pos, mix_ref, scale_ref,
                 store, gw)


def _pool_sample(u3, hist3, pool_mix, pool_scale, past_len):
    nb, t, pw = u3.shape
    ng, gw, _ = pool_mix.shape
    return pl.pallas_call(
        functools.partial(_pool_sample_kernel, nb=nb, t=t, gw=gw, past_len=past_len),
        grid=(1,),
        in_specs=[pl.BlockSpec((nb, POOL_HIST, pw), lambda i: (0, 0, 0)),
                  pl.BlockSpec((nb, t, pw), lambda i: (0, 0, 0)),
                  pl.BlockSpec((ng, gw, gw), lambda i: (0, 0, 0)),
                  pl.BlockSpec((1, pw), lambda i: (0, 0))],
        out_specs=pl.BlockSpec((nb * t, pw), lambda i: (0, 0)),
        out_shape=jax.ShapeDtypeStruct((nb * t, pw), BF16),
        scratch_shapes=[pltpu.VMEM((nb, POOL_HIST + t, pw), F32)],
        compiler_params=_cparams(1, 32),
        name="pool_sample",
    )(hist3, u3, pool_mix, pool_scale.reshape(1, pw))


def _softplus2(y):
    return jnp.maximum(y, 0.0) + jnp.log2(1.0 + jnp.exp2(-jnp.abs(y)))


def _suffix_matrix(nk, nrep):
    r = lax.broadcasted_iota(jnp.int32, (nk, nk + nrep), 0)
    c = lax.broadcasted_iota(jnp.int32, (nk, nk + nrep), 1)
    return jnp.where((r > c) | (c >= nk), 1.0, 0.0).astype(BF16)


def _attn_prompt_kernel(bias_ref, q_ref, k_ref, v_ref, o_ref, acc_ref, car_ref, ls_ref, r_ref,
                        *, tq, tk):
    h = pl.program_id(0)
    i = pl.program_id(1)
    bias = bias_ref[h]
    q = q_ref[...]
    nsub = tq // tk
    sufm = _suffix_matrix(tk, HEAD_DIM)
    row = lax.broadcasted_iota(jnp.int32, (tq, tk), 0)
    col = lax.broadcasted_iota(jnp.int32, (tq, tk), 1)

    def sub_blocks(j):
        for u in range(nsub):
            off = (nsub - 1 - u) * tk
            yield u, off, pl.multiple_of(j * tq + off, tk)

    def scores(j, masked, slot):
        for u, off, start in sub_blocks(j):
            y = lax.dot_general(q, k_ref[pl.ds(start, tk), :], (((1,), (1,)), ((), ())),
                                preferred_element_type=F32) + bias
            s2 = _softplus2(y)
            lk = jnp.where(col + off < row, s2, 0.0) if masked else s2
            r_ref[slot, u] = jnp.dot(lk.astype(BF16), sufm, preferred_element_type=F32)
            ls_ref[slot, u] = y - s2

    def weights(j, masked, slot):
        car = car_ref[...]
        acc = acc_ref[...]
        for u, off, start in sub_blocks(j):
            a = jnp.exp2(ls_ref[slot, u] - r_ref[slot, u, :, 0:tk]
                         - jnp.concatenate([car] * (tk // HEAD_DIM), axis=1))
            if masked:
                a = jnp.where(col + off < row, a, 0.0)
            acc = acc + jnp.dot(a.astype(BF16), v_ref[pl.ds(start, tk), :],
                                preferred_element_type=F32)
            car = car + r_ref[slot, u, :, tk:tk + HEAD_DIM]
        acc_ref[...] = acc
        car_ref[...] = car

    acc_ref[...] = jnp.zeros_like(acc_ref)
    car_ref[...] = jnp.zeros_like(car_ref)
    scores(i, True, 0)

    @pl.when(i == 0)
    def _():
        weights(0, True, 0)

    @pl.when(i > 0)
    def _():
        weights(i, True, 0)
        scores(i - 1, False, 1)

        def body(s, c):
            j = i - 1 - s
            slot = (s + 1) & 1
            weights(j, False, slot)
            scores(j - 1, False, 1 - slot)
            return c

        lax.fori_loop(0, i - 1, body, 0)
        weights(0, False, i & 1)

    o_ref[...] = acc_ref[...].astype(o_ref.dtype)


def _attn_prompt(qs, kb, vb, bias2, tq=ATTN_TQ, tk=ATTN_TK):
    s, aw = qs.shape
    nh = aw // HEAD_DIM
    nsub = tq // tk
    return pl.pallas_call(
        functools.partial(_attn_prompt_kernel, tq=tq, tk=tk),
        grid_spec=pltpu.PrefetchScalarGridSpec(
            num_scalar_prefetch=1,
            grid=(nh, s // tq),
            in_specs=[pl.BlockSpec((tq, HEAD_DIM), lambda h, i, b: (i, h)),
                      pl.BlockSpec((s, HEAD_DIM), lambda h, i, b: (0, h)),
                      pl.BlockSpec((s, HEAD_DIM), lambda h, i, b: (0, h))],
            out_specs=pl.BlockSpec((tq, HEAD_DIM), lambda h, i, b: (i, h)),
            scratch_shapes=[pltpu.VMEM((tq, HEAD_DIM), F32), pltpu.VMEM((tq, HEAD_DIM), F32),
                            pltpu.VMEM((2, nsub, tq, tk), F32),
                            pltpu.VMEM((2, nsub, tq, tk + HEAD_DIM), F32)]),
        out_shape=jax.ShapeDtypeStruct((s, aw), BF16),
        compiler_params=_cparams(2, 32),
        name="attn_prompt",
    )(bias2, qs, kb, vb)


def _attn_sample_kernel(pt_ref, bias_ref, q_ref, kn_ref, vn_ref, ck_ref, cv_ref, o_ref,
                        acc_ref, car_ref, kpad_ref, vpad_ref, kbuf, vbuf, sem,
                        *, pps, nh, t, skew, n_pages):
    b = pl.program_id(0)
    c = pl.program_id(1)
    nc = pl.num_programs(1)
    step = b * nc + c
    slot = step & 1
    ht = nh * t
    page = HEAD_DIM
    sufm = _suffix_matrix(page, HEAD_DIM)
    q = q_ref[...]
    head = lambda h: slice(h * HEAD_DIM, (h + 1) * HEAD_DIM)
    rows = lambda h: slice(h * t, (h + 1) * t)
    qall = jnp.concatenate([q[:, head(h)] for h in range(nh)], axis=0).astype(BF16)
    bias_rows = jnp.concatenate([jnp.full((t, page), bias_ref[h], F32) for h in range(nh)], axis=0)
    kpos = lax.broadcasted_iota(jnp.int32, (ht, page), 1)
    qpos = lax.broadcasted_iota(jnp.int32, (ht, page), 0) & (t - 1)

    def page_copies(bb, cc, sl):
        cps = []
        for i in range(pps):
            pg = pt_ref[bb, n_pages - 1 - (cc * pps + i)]
            for h in range(nh):
                cps.append(pltpu.make_async_copy(ck_ref.at[pg, :, h, :], kbuf.at[sl, i, h], sem.at[sl]))
                cps.append(pltpu.make_async_copy(cv_ref.at[pg, :, h, :], vbuf.at[sl, i, h], sem.at[sl]))
        return cps

    def scores(k3, masked):
        k2 = k3.reshape(nh * page, HEAD_DIM).astype(BF16)
        z = lax.dot_general(qall, k2, (((1,), (1,)), ((), ())), preferred_element_type=F32)
        y = jnp.concatenate([z[rows(h), h * page:(h + 1) * page] for h in range(nh)], axis=0) + bias_rows
        s2 = _softplus2(y)
        lk = jnp.where(kpos < qpos, s2, 0.0) if masked else s2
        return y - s2, jnp.dot(lk.astype(BF16), sufm, preferred_element_type=F32)

    def weights(v3, ls, r, car, acc, masked):
        a = jnp.exp2(ls - r[:, :page] - car)
        if masked:
            a = jnp.where(kpos < qpos, a, 0.0)
        zero = jnp.zeros((t, page), F32)
        a_bd = jnp.concatenate(
            [jnp.concatenate([a[rows(h)] if g == h else zero for g in range(nh)], axis=1)
             for h in range(nh)], axis=0).astype(BF16)
        v2 = v3.reshape(nh * page, HEAD_DIM).astype(BF16)
        return car + r[:, page:], acc + jnp.dot(a_bd, v2, preferred_element_type=F32)

    def process(pages, masked):
        car = car_ref[...]
        acc = acc_ref[...]
        staged = []
        for idx in range(len(pages) + skew):
            if idx < len(pages):
                staged.append(scores(pages[idx][0](), masked))
            if idx >= skew:
                car, acc = weights(pages[idx - skew][1](), *staged[idx - skew], car, acc, masked)
        car_ref[...] = car
        acc_ref[...] = acc

    def start_all(cps):
        for n, cp in enumerate(cps):
            cp.start(priority=n & 1)

    @pl.when(step == 0)
    def _():
        start_all(page_copies(0, 0, 0))

    @pl.when(step + 1 < pl.num_programs(0) * nc)
    def _():
        wrap = c + 1 == nc
        start_all(page_copies(jnp.where(wrap, b + 1, b), jnp.where(wrap, 0, c + 1), 1 - slot))

    for cp in page_copies(b, c, slot):
        cp.wait()

    @pl.when(c == 0)
    def _():
        acc_ref[...] = jnp.zeros_like(acc_ref)
        car_ref[...] = jnp.zeros_like(car_ref)
        kpad_ref[...] = jnp.zeros_like(kpad_ref)
        vpad_ref[...] = jnp.zeros_like(vpad_ref)
        for h in range(nh):
            kpad_ref[h, 0:t, :] = kn_ref[:, head(h)]
            vpad_ref[h, 0:t, :] = vn_ref[:, head(h)]
        process([(lambda: kpad_ref[...], lambda: vpad_ref[...])], True)

    process([(functools.partial(lambda i: kbuf[slot, i], i), functools.partial(lambda i: vbuf[slot, i], i))
             for i in range(pps)], False)

    @pl.when(c == nc - 1)
    def _():
        acc = acc_ref[...]
        for h in range(nh):
            o_ref[:, head(h)] = acc[rows(h)]


def _attn_sample(q, kn, vn, cache_k, cache_v, page_table, bias2, t, pps=PAGES_PER_STEP):
    nbt, aw = q.shape
    nb = nbt // t
    nh = aw // HEAD_DIM
    n_pages = page_table.shape[1]
    page = cache_k.shape[1]
    assert cache_k.shape[1:] == (HEAD_DIM, nh, HEAD_DIM) and n_pages % pps == 0
    row_spec = pl.BlockSpec((t, aw), lambda b, c, pt, bias: (b, 0))
    any_spec = pl.BlockSpec(memory_space=pl.ANY)
    return pl.pallas_call(
        functools.partial(_attn_sample_kernel, pps=pps, nh=nh, t=t, skew=PAGE_SKEW, n_pages=n_pages),
        grid_spec=pltpu.PrefetchScalarGridSpec(
            num_scalar_prefetch=2,
            grid=(nb, n_pages // pps),
            in_specs=[row_spec, row_spec, row_spec, any_spec, any_spec],
            out_specs=row_spec,
            scratch_shapes=[pltpu.VMEM((nh * t, HEAD_DIM), F32), pltpu.VMEM((nh * t, HEAD_DIM), F32),
                            pltpu.VMEM((nh, page, HEAD_DIM), F32),
                            pltpu.VMEM((nh, page, HEAD_DIM), F32),
                            pltpu.VMEM((2, pps, nh, page, HEAD_DIM), F32),
                            pltpu.VMEM((2, pps, nh, page, HEAD_DIM), F32),
                            pltpu.SemaphoreType.DMA((2,))]),
        out_shape=jax.ShapeDtypeStruct((nbt, aw), F32),
        compiler_params=_cparams(2, 40),
        name="attn_sample",
    )(page_table, bias2, q, kn, vn, cache_k, cache_v)


def _merge_kernel(po_ref, at_ref, gt_ref, x_ref, g1_ref, n2_ref, sc_ref, sh_ref,
                  wbp_ref, wba_ref, wo_ref, *refs, d, n_blocks):
    x1_ref, h2_ref = refs[-2:]
    i = pl.program_id(0)

    @pl.when(i < n_blocks)
    def _():
        bp = jnp.dot(po_ref[...], wbp_ref[...], preferred_element_type=F32)
        ba = jnp.dot(at_ref[...].astype(BF16), wba_ref[...], preferred_element_type=F32)
        merged = gt_ref[:, :d] * bp + gt_ref[:, d:] * ba
        x1 = x_ref[...] + g1_ref[...] * jnp.dot(merged.astype(BF16), wo_ref[...],
                                                preferred_element_type=F32)
        x1_ref[...] = x1
        h2_ref[...] = _rms(x1, n2_ref[...]) * (1.0 + sc_ref[...]) + sh_ref[...]

    @pl.when(i >= n_blocks)
    def _():
        h2_ref[...] = jnp.zeros_like(h2_ref)


def _merge(pool_out, attn, gates, x, gate1, norm2_g, scale2, shift2, wbp, wba, wo, tm,
           n_all, row0, h2_prev=None):
    n, d = x.shape
    pw, aw = pool_out.shape[1], attn.shape[1]
    rb0 = row0 // tm
    n_blocks = n // tm
    steps = n_blocks if h2_prev is not None else (n_all - row0) // tm
    blk = lambda i: jnp.minimum(i, n_blocks - 1)
    row = lambda w: pl.BlockSpec((tm, w), lambda i: (blk(i), 0))
    mod = lambda m: (pl.BlockSpec((1, d), lambda i: (0, 0)) if m.shape[0] == 1 else row(d))
    in_specs = [row(pw), row(aw), row(2 * d), row(d), mod(gate1),
                pl.BlockSpec((1, d), lambda i: (0, 0)), mod(scale2), mod(shift2),
                _resident(wbp.shape), _resident(wba.shape), _resident(wo.shape)]
    args = [pool_out, attn, gates, x, gate1, norm2_g.reshape(1, d), scale2, shift2, wbp, wba, wo]
    aliases = {}
    if h2_prev is not None:
        in_specs.append(pl.BlockSpec(memory_space=pl.ANY))
        args.append(h2_prev)
        aliases = {len(args) - 1: 1}
    return pl.pallas_call(
        functools.partial(_merge_kernel, d=d, n_blocks=n_blocks),
        grid=(steps,),
        in_specs=in_specs,
        out_specs=[row(d), pl.BlockSpec((tm, d), lambda i: (rb0 + i, 0))],
        out_shape=[jax.ShapeDtypeStruct((n, d), F32), jax.ShapeDtypeStruct((n_all, d), F32)],
        input_output_aliases=aliases,
        compiler_params=_cparams(1, 56),
        name="merge",
    )(*args)


def _first_max(v, ids, n):
    m = jnp.max(v, axis=0, keepdims=True)
    idx = jnp.min(jnp.where(v == m, ids, n), axis=0, keepdims=True)
    return m, idx


def _router_kernel(h_ref, wr_ref, rb_ref, ei_ref, wt_ref, *, ne, ng, tkg, topk):
    epg = ne // ng
    hb = h_ref[...].astype(BF16)
    logits = lax.dot_general(wr_ref[...], hb, (((1,), (1,)), ((), ())), preferred_element_type=F32)
    scores = jax.nn.sigmoid(logits)
    sel = scores + rb_ref[...]
    tm = sel.shape[1]
    neg = -jnp.inf
    fiota = lambda n: lax.broadcasted_iota(jnp.int32, (n, tm), 0).astype(F32)
    ids_g = fiota(epg)
    gs = []
    for g in range(ng):
        v = sel[g * epg:(g + 1) * epg, :]
        m1, i1 = _first_max(v, ids_g, float(epg))
        m2 = jnp.max(jnp.where(ids_g == i1, neg, v), axis=0, keepdims=True)
        gs.append(m1 + m2)
    gsc = jnp.concatenate(gs, axis=0)
    ids_n = fiota(ng)
    gmask = jnp.zeros((ng, tm), F32)
    for _ in range(tkg):
        _, gi = _first_max(gsc, ids_n, float(ng))
        hit = ids_n == gi
        gmask = jnp.where(hit, 1.0, gmask)
        gsc = jnp.where(hit, neg, gsc)
    emask = jnp.concatenate(
        [jnp.broadcast_to(gmask[g:g + 1, :], (epg, tm)) for g in range(ng)], axis=0)
    cand = jnp.where(emask > 0.5, sel, neg)
    ids_e = fiota(ne)
    eis, wts = [], []
    for _ in range(topk):
        _, ei = _first_max(cand, ids_e, float(ne))
        hit = ids_e == ei
        eis.append(ei)
        wts.append(jnp.sum(jnp.where(hit, scores, 0.0), axis=0, keepdims=True))
        cand = jnp.where(hit, neg, cand)
    w = jnp.concatenate(wts, axis=0)
    ei_ref[...] = jnp.concatenate(eis, axis=0).astype(jnp.int32)
    wt_ref[...] = w / jnp.sum(w, axis=0, keepdims=True) * ROUTED_SCALE


def _router(h2, wr_t, router_bias, tm):
    n, d = h2.shape
    ne = wr_t.shape[0]
    return pl.pallas_call(
        functools.partial(_router_kernel, ne=ne, ng=N_EXPERT_GROUPS, tkg=TOPK_GROUPS, topk=TOP_K),
        grid=(n // tm,),
        in_specs=[pl.BlockSpec((tm, d), lambda i: (i, 0)),
                  pl.BlockSpec((ne, d), lambda i: (0, 0)),
                  pl.BlockSpec((ne, 1), lambda i: (0, 0))],
        out_specs=[pl.BlockSpec((TOP_K, tm), lambda i: (0, i)),
                   pl.BlockSpec((TOP_K, tm), lambda i: (0, i))],
        out_shape=[jax.ShapeDtypeStruct((TOP_K, n), jnp.int32),
                   jax.ShapeDtypeStruct((TOP_K, n), F32)],
        compiler_params=_cparams(1, 32),
        name="router",
    )(h2, wr_t, router_bias.reshape(ne, 1))


def _expert_kernel(be_ref, nu_ref, tok_ref, tokn_ref, dstp_ref, dst_ref, h_ref, wg_ref, wu_ref,
                   wd_ref, y_ref, xbuf, ybuf, wgb, wub, wdb, gsem, ssem,
                   *, tm, unroll, pad_row0, n_chunk):
    b = pl.program_id(0)
    nu = nu_ref[0]
    slot = b & 1
    other = 1 - slot
    ed = wg_ref.shape[1]
    GATHER_PRIORITY, SCATTER_PRIORITY = 0, 1

    def gather(tref, s, r):
        return pltpu.make_async_copy(h_ref.at[pl.ds(tref[0, 0, r], 1)], xbuf.at[s, pl.ds(r, 1)],
                                     gsem.at[s])

    def scatter(row, s, r):
        return pltpu.make_async_copy(ybuf.at[s, pl.ds(r, 1)], y_ref.at[pl.ds(row, 1)], ssem.at[s])

    def wait_gather(s):
        pltpu.make_async_copy(h_ref.at[pl.ds(0, tm)], xbuf.at[s], gsem.at[s]).wait()

    def wait_scatter(s):
        pltpu.make_async_copy(ybuf.at[s], y_ref.at[pl.ds(0, tm)], ssem.at[s]).wait()

    def rolled(start_row):
        def body(i, c):
            for u in range(unroll):
                start_row(i * unroll + u)
            return c

        lax.fori_loop(0, tm // unroll, body, 0)

    @pl.when(b < nu)
    def _():
        @pl.when(b == 0)
        def _():
            ybuf[...] = jnp.zeros_like(ybuf)
            rolled(lambda r: gather(tok_ref, 0, r).start(GATHER_PRIORITY))
            rolled(lambda r: scatter(pad_row0 + r, 0, r).start(SCATTER_PRIORITY))

        @pl.when((b == 0) | (be_ref[b] != be_ref[jnp.maximum(b - 1, 0)]))
        def _():
            wgb[...] = wg_ref[...].astype(BF16)
            wub[...] = wu_ref[...].astype(BF16)
            wdb[...] = wd_ref[...].astype(BF16)

        wait_gather(slot)
        starts = ([functools.partial(lambda r: gather(tokn_ref, other, r).start(GATHER_PRIORITY), r)
                   for r in range(tm)]
                  + [functools.partial(
                      lambda r: scatter(dstp_ref[0, 0, r], other, r).start(SCATTER_PRIORITY), r)
                     for r in range(tm)])
        n_groups = 3 * n_chunk
        per = -(-len(starts) // n_groups)

        def issue_group(g):
            for st in starts[g * per:(g + 1) * per]:
                st()

        x = xbuf[slot].astype(BF16)
        cw = ed // n_chunk
        acts = []
        for c in range(n_chunk):
            cs = slice(c * cw, (c + 1) * cw)
            gate = jnp.dot(x, wgb[:, cs], preferred_element_type=F32)
            issue_group(2 * c)
            up = jnp.dot(x, wub[:, cs], preferred_element_type=F32)
            issue_group(2 * c + 1)
            acts.append((_silu(gate) * up).astype(BF16))
        y = None
        for c in range(n_chunk):
            part = jnp.dot(acts[c], wdb[c * cw:(c + 1) * cw, :], preferred_element_type=F32)
            y = part if y is None else y + part
            issue_group(2 * n_chunk + c)
        wait_scatter(slot)
        ybuf[slot] = y

        @pl.when(b == nu - 1)
        def _():
            wait_gather(other)
            rolled(lambda r: scatter(dst_ref[0, 0, r], slot, r).start(SCATTER_PRIORITY))
            wait_scatter(other)
            wait_scatter(slot)


def _experts(block_e, n_used, slot_tok, slot_dst, h2, w_gate, w_up, w_down, tm=MOE_TM):
    nb = block_e.shape[0]
    n_all, d = h2.shape
    ed = w_gate.shape[2]
    n_rows = n_all * TOP_K
    cur = pl.BlockSpec((1, 1, tm), lambda b, be, nu: (b, 0, 0), memory_space=pltpu.SMEM)
    nxt = pl.BlockSpec((1, 1, tm), lambda b, be, nu: (jnp.minimum(b + 1, nb - 1), 0, 0),
                       memory_space=pltpu.SMEM)
    any_spec = pl.BlockSpec(memory_space=pl.ANY)
    tok3 = slot_tok.reshape(nb, 1, tm)
    dst3 = slot_dst.reshape(nb, 1, tm)
    pad1 = (n_rows + tm + jnp.arange(tm, dtype=jnp.int32)).reshape(1, 1, tm)
    dstp3 = jnp.concatenate([pad1, dst3[:-1]], axis=0)
    return pl.pallas_call(
        functools.partial(_expert_kernel, tm=tm, unroll=MOE_DMA_UNROLL, pad_row0=n_rows,
                          n_chunk=MOE_N_CHUNK),
        grid_spec=pltpu.PrefetchScalarGridSpec(
            num_scalar_prefetch=2,
            grid=(nb,),
            in_specs=[cur, nxt, cur, cur, any_spec,
                      pl.BlockSpec((None, d, ed), lambda b, be, nu: (be[b], 0, 0)),
                      pl.BlockSpec((None, d, ed), lambda b, be, nu: (be[b], 0, 0)),
                      pl.BlockSpec((None, ed, d), lambda b, be, nu: (be[b], 0, 0))],
            out_specs=any_spec,
            scratch_shapes=[pltpu.VMEM((2, tm, d), F32), pltpu.VMEM((2, tm, d), F32),
                            pltpu.VMEM((d, ed), BF16), pltpu.VMEM((d, ed), BF16),
                            pltpu.VMEM((ed, d), BF16),
                            pltpu.SemaphoreType.DMA((2,)), pltpu.SemaphoreType.DMA((2,))]),
        out_shape=jax.ShapeDtypeStruct((n_rows + 2 * tm, d), F32),
        compiler_params=_cparams(1, 56),
        name="experts",
    )(block_e, n_used, tok3, tok3, dstp3, dst3, h2, w_gate, w_up, w_down)


def _dispatch_meta(eidx, n_experts, tm):
    n = eidx.shape[0]
    nk = n * TOP_K
    nb = -(-nk // tm) + n_experts
    bits = max(nk - 1, 1).bit_length()
    assert n_experts << bits < 2 ** 31
    keys = jnp.sort((eidx.reshape(-1) << bits) | jnp.arange(nk, dtype=jnp.int32))
    order = keys & ((1 << bits) - 1)
    bounds = jnp.searchsorted(keys, jnp.arange(n_experts + 1, dtype=jnp.int32) << bits).astype(jnp.int32)
    starts, counts = bounds[:-1], bounds[1:] - bounds[:-1]
    padded = (counts + tm - 1) // tm * tm
    pends = jnp.cumsum(padded)
    pstarts = pends - padded
    n_used = (pends[-1] // tm).astype(jnp.int32)
    blk = jnp.arange(nb, dtype=jnp.int32)
    blk_start = blk * tm
    be = jnp.minimum(jnp.sum((pends[None, :] <= blk_start[:, None]).astype(jnp.int32), axis=1),
                     n_experts - 1)
    be = jnp.where(blk < n_used, be, be[jnp.maximum(n_used - 1, 0)]).astype(jnp.int32)
    within = (blk_start - pstarts[be])[:, None] + jnp.arange(tm, dtype=jnp.int32)[None, :]
    valid = within < counts[be][:, None]
    flat = order[jnp.clip(starts[be][:, None] + within, 0, nk - 1)]
    tok = flat // TOP_K
    pad_row = nk + (blk & 1)[:, None] * tm + jnp.arange(tm, dtype=jnp.int32)[None, :]
    dst = jnp.where(valid, (flat % TOP_K) * n + tok, pad_row)
    return be, n_used.reshape(1), tok.astype(jnp.int32), dst.astype(jnp.int32)


def _combine_kernel(*refs, d, sd):
    y_refs = refs[:TOP_K]
    wt_ref, h_ref, x_ref, g2_ref, fg_ref, wgu_ref, wd_ref, o_ref = refs[TOP_K:]
    wt = wt_ref[...]
    routed = y_refs[0][...] * wt[:, 0:1]
    for k in range(1, TOP_K):
        routed = routed + y_refs[k][...] * wt[:, k:k + 1]
    hb = h_ref[...].astype(BF16)
    gu = jnp.dot(hb, wgu_ref[...], preferred_element_type=F32)
    act = (_silu(gu[:, :sd]) * gu[:, sd:]).astype(BF16)
    ffn = routed + jnp.dot(act, wd_ref[...], preferred_element_type=F32)
    x2 = x_ref[...] + g2_ref[...] * ffn
    o_ref[...] = _rms(x2, fg_ref[...])


def _combine(y, n_all, row0, wts, h2, x1, gate2, final_g, ws_gu, ws_d, tm):
    n, d = x1.shape
    sd = ws_d.shape[0]
    rb0 = row0 // tm
    kb = n_all // tm
    row = lambda w: pl.BlockSpec((tm, w), lambda i: (i, 0))
    shifted = lambda w: pl.BlockSpec((tm, w), lambda i: (rb0 + i, 0))
    y_specs = [pl.BlockSpec((tm, d), functools.partial(lambda i, k: (k * kb + rb0 + i, 0), k=k))
               for k in range(TOP_K)]
    return pl.pallas_call(
        functools.partial(_combine_kernel, d=d, sd=sd),
        grid=(n // tm,),
        in_specs=y_specs + [shifted(TOP_K), shifted(d), row(d), _mod_spec(gate2, tm, d),
                            pl.BlockSpec((1, d), lambda i: (0, 0)),
                            _resident(ws_gu.shape), _resident(ws_d.shape)],
        out_specs=row(d),
        out_shape=jax.ShapeDtypeStruct((n, d), F32),
        compiler_params=_cparams(1, 48),
        name="combine",
    )(*([y] * TOP_K), wts, h2, x1, gate2, final_g.reshape(1, d), ws_gu, ws_d)


def _in_proj(h, w_in, pw, aw, d, tm, q_dtype, want_bf16_kv):
    tn = 1024
    ident = lambda acc: (acc,)
    (u,) = _mm(h, w_in, 0, pw, [F32], ident, tm, tn, "proj_u")
    qscale = LOG2E / math.sqrt(HEAD_DIM)
    (q,) = _mm(h, w_in, pw, aw, [q_dtype], lambda acc: (acc * qscale,), tm, tn, "proj_q")
    kv_dt = [F32, BF16] if want_bf16_kv else [F32]
    both = (lambda acc: (acc, acc)) if want_bf16_kv else ident
    k = _mm(h, w_in, pw + aw, aw, kv_dt, both, tm, tn, "proj_k")
    v = _mm(h, w_in, pw + 2 * aw, aw, kv_dt, both, tm, tn, "proj_v")
    (gates,) = _mm(h, w_in, pw + 3 * aw, 2 * d, [F32], lambda acc: (jax.nn.sigmoid(acc),),
                   tm, tn, "proj_gates")
    return u, q, k, v, gates


def kernel(x_prompt, x_sample, c_prompt, c_sample, state_pool, cache_k, cache_v, page_table,
           norm1_g, norm2_g, w_ada, b_ada, w_in, pool_mix, pool_scale, w_br_pool, w_br_attn,
           sb_bias, w_o, w_router, router_bias, w_e_gate, w_e_up, w_e_down, w_s_gate, w_s_up,
           w_s_down, final_g):
    depth = w_in.shape[0]
    nbp, seq, d = x_prompt.shape
    nbs, t, _ = x_sample.shape
    assert nbp == 1 and depth == 1 and t & (t - 1) == 0
    l = 0
    pw = pool_scale.shape[1]
    aw = w_br_attn.shape[1]
    nh = aw // HEAD_DIM
    n_experts = w_router.shape[2]
    n_pages, page = page_table.shape[1], cache_k.shape[2]
    past_len = n_pages * page
    ns = nbs * t
    n_all = seq + ns
    buf = POOL_HIST - 1

    xp = x_prompt.reshape(seq, d)
    xs = x_sample.reshape(ns, d)
    rows = nbp + nbs
    rpad = -(-rows // 8) * 8
    c_all = jnp.concatenate([c_prompt, c_sample, jnp.zeros((rpad - rows, d), F32)], axis=0)

    mod = _ada(c_all, w_ada[l], b_ada[l])
    mod_p = [mod[0:1, i * d:(i + 1) * d] for i in range(N_MOD)]
    mod_s = [jnp.repeat(mod[nbp:rows, i * d:(i + 1) * d], t, axis=0) for i in range(N_MOD)]
    wbp, wba, wo = (w.astype(BF16) for w in (w_br_pool[l], w_br_attn[l], w_o[l]))
    wr_t = w_router[l].T.astype(BF16)
    ws_gu = jnp.concatenate([w_s_gate[l], w_s_up[l]], axis=1).astype(BF16)
    ws_d = w_s_down[l].astype(BF16)
    bias2 = sb_bias[l] * LOG2E

    hp = _normmod(xp, norm1_g[l], mod_p[1], mod_p[0], 512)
    u_p, q_p, (k_p, kb_p), (v_p, vb_p), gates_p = _in_proj(hp, w_in[l], pw, aw, d, 1024, BF16, True)
    pool_p = _pool_prompt(u_p, pool_mix[l], pool_scale[l])
    attn_p = _attn_prompt(q_p, kb_p, vb_p, bias2)
    x1_p, h2 = _merge(pool_p, attn_p, gates_p, xp, mod_p[2], norm2_g[l], mod_p[4], mod_p[3],
                      wbp, wba, wo, 256, n_all, 0)

    hs = _normmod(xs, norm1_g[l], mod_s[1], mod_s[0], ns)
    u_s, q_s, (k_s,), (v_s,), gates_s = _in_proj(hs, w_in[l], pw, aw, d, ns, F32, False)
    hist_s = jnp.concatenate([jnp.zeros((nbs, 1, pw), F32), state_pool[l]], axis=1)
    pool_s = _pool_sample(u_s.reshape(nbs, t, pw), hist_s, pool_mix[l], pool_scale[l], past_len)
    attn_s = _attn_sample(q_s, k_s, v_s, cache_k[l], cache_v[l], page_table, bias2, t)
    x1_s, h2 = _merge(pool_s, attn_s, gates_s, xs, mod_s[2], norm2_g[l], mod_s[4], mod_s[3],
                      wbp, wba, wo, ns, n_all, seq, h2_prev=h2)

    ei, wt = _router(h2, wr_t, router_bias[l], 768)
    be, n_used, slot_tok, slot_dst = _dispatch_meta(ei.T, n_experts, MOE_TM)
    y = _experts(be, n_used, slot_tok, slot_dst, h2, w_e_gate[l], w_e_up[l], w_e_down[l])
    wts = wt.T
    yp = _combine(y, n_all, 0, wts, h2, x1_p, mod_p[5], final_g, ws_gu, ws_d, 128)
    ys = _combine(y, n_all, seq, wts, h2, x1_s, mod_s[5], final_g, ws_gu, ws_d, 128)

    heads = lambda a, b_: a.reshape(1, b_, -1, nh, HEAD_DIM)
    pool_state_s = jnp.concatenate([state_pool[l][:, t:], u_s.reshape(nbs, t, pw)], axis=1)
    return (yp.reshape(nbp, seq, d), ys.reshape(nbs, t, d),
            u_p[seq - buf:].reshape(1, nbp, buf, pw), heads(k_p, nbp), heads(v_p, nbp),
            pool_state_s[None], heads(k_s, nbs), heads(v_s, nbs))
```

```python
import functools
import math

import jax
import jax.numpy as jnp
from jax import lax
from jax.experimental import pallas as pl
from jax.experimental.pallas import tpu as pltpu

F32 = jnp.float32
BF16 = jnp.bfloat16

HEAD_DIM = 128
POOL_WINDOWS = (2, 4, 8, 16)
POOL_HIST = 16
N_EXPERT_GROUPS = 8
TOPK_GROUPS = 4
TOP_K = 8
ROUTED_SCALE = 2.5
RMS_EPS = 1e-6
N_MOD = 6
LOG2E = 1.4426950408889634

MOE_TM = 256
MOE_DMA_UNROLL = 8
MOE_N_CHUNK = 2
ATTN_TQ = 512
ATTN_TK = 256
PAGES_PER_STEP = 8
PAGE_SKEW = 2
MIB = 1024 * 1024


def _cparams(n_axes, vmem_mib):
    return pltpu.CompilerParams(dimension_semantics=("arbitrary",) * n_axes,
                                vmem_limit_bytes=vmem_mib * MIB)


def _silu(x):
    return x * jax.nn.sigmoid(x)


def _resident(shape):
    return pl.BlockSpec(shape, lambda *_: (0,) * len(shape), pipeline_mode=pl.Buffered(1))


def _ada_kernel(c_ref, w_ref, b_ref, o_ref):
    s = _silu(c_ref[...]).astype(BF16)
    o_ref[...] = jnp.dot(s, w_ref[...].astype(BF16), preferred_element_type=F32) + b_ref[...]


def _ada(c_all, w_ada, b_ada, tn=512):
    r, d = c_all.shape
    n = w_ada.shape[1]
    return pl.pallas_call(
        _ada_kernel,
        grid=(n // tn,),
        in_specs=[pl.BlockSpec((r, d), lambda j: (0, 0)),
                  pl.BlockSpec((d, tn), lambda j: (0, j)),
                  pl.BlockSpec((1, tn), lambda j: (0, j))],
        out_specs=pl.BlockSpec((r, tn), lambda j: (0, j)),
        out_shape=jax.ShapeDtypeStruct((r, n), F32),
        compiler_params=_cparams(1, 32),
        name="ada_mod",
    )(c_all, w_ada, b_ada.reshape(1, n))


def _rms(x, g):
    return x * lax.rsqrt(jnp.mean(x * x, axis=-1, keepdims=True) + RMS_EPS) * g


def _normmod_kernel(x_ref, g_ref, sc_ref, sh_ref, o_ref):
    xn = _rms(x_ref[...], g_ref[...])
    o_ref[...] = (xn * (1.0 + sc_ref[...]) + sh_ref[...]).astype(o_ref.dtype)


def _mod_spec(mod, tm, d):
    if mod.shape[0] == 1:
        return pl.BlockSpec((1, d), lambda i: (0, 0))
    return pl.BlockSpec((tm, d), lambda i: (i, 0))


def _normmod(x, g, scale, shift, tm):
    n, d = x.shape
    return pl.pallas_call(
        _normmod_kernel,
        grid=(n // tm,),
        in_specs=[pl.BlockSpec((tm, d), lambda i: (i, 0)),
                  pl.BlockSpec((1, d), lambda i: (0, 0)),
                  _mod_spec(scale, tm, d), _mod_spec(shift, tm, d)],
        out_specs=pl.BlockSpec((tm, d), lambda i: (i, 0)),
        out_shape=jax.ShapeDtypeStruct((n, d), BF16),
        compiler_params=_cparams(1, 32),
        name="norm_mod",
    )(x, g.reshape(1, d), scale, shift)


def _mm_kernel(a_ref, w_ref, *refs, epilogue, n_out):
    out_refs, wb_ref = refs[:n_out], refs[n_out]

    @pl.when(pl.program_id(1) == 0)
    def _():
        wb_ref[...] = w_ref[...].astype(BF16)

    acc = jnp.dot(a_ref[...], wb_ref[...], preferred_element_type=F32)
    for r, v in zip(out_refs, epilogue(acc)):
        r[...] = v.astype(r.dtype)


def _mm(a, w, col0, ncols, out_dtypes, epilogue, tm, tn, name):
    m, k = a.shape
    cb0 = col0 // tn
    out_spec = pl.BlockSpec((tm, tn), lambda j, i: (i, j))
    return pl.pallas_call(
        functools.partial(_mm_kernel, epilogue=epilogue, n_out=len(out_dtypes)),
        grid=(ncols // tn, m // tm),
        in_specs=[pl.BlockSpec((tm, k), lambda j, i: (i, 0)),
                  pl.BlockSpec((k, tn), lambda j, i: (0, cb0 + j))],
        out_specs=[out_spec] * len(out_dtypes),
        out_shape=[jax.ShapeDtypeStruct((m, ncols), dt) for dt in out_dtypes],
        scratch_shapes=[pltpu.VMEM((k, tn), BF16)],
        compiler_params=_cparams(2, 56),
        name=name,
    )(a, w)


def _pool_groups(win_fn, u_fn, pos, mix_ref, scale_ref, store_fn, gw):
    for g, w in enumerate(POOL_WINDOWS):
        lo = g * gw
        win = win_fn(0, lo)
        for dlt in range(1, w):
            win = win + win_fn(dlt, lo)
        cnt = jnp.minimum(pos + 1, w).astype(F32)
        dd = win / cnt - u_fn(lo)
        mixed = jnp.dot(dd.astype(BF16), mix_ref[g].astype(BF16), preferred_element_type=F32)
        store_fn(lo, mixed * scale_ref[:, lo:lo + gw])


def _pool_prompt_kernel(hist_ref, u_ref, mix_ref, scale_ref, o_ref, ext_ref, *, tm, gw):
    i = pl.program_id(0)
    ext_ref[0:POOL_HIST, :] = jnp.where(i == 0, 0.0, hist_ref[...])
    ext_ref[POOL_HIST:POOL_HIST + tm, :] = u_ref[...]
    pos = i * tm + lax.broadcasted_iota(jnp.int32, (tm, 1), 0)

    def store(lo, val):
        o_ref[:, lo:lo + gw] = val.astype(o_ref.dtype)

    _pool_groups(lambda dlt, lo: ext_ref[pl.ds(POOL_HIST - dlt, tm), lo:lo + gw],
                 lambda lo: u_ref[:, lo:lo + gw], pos, mix_ref, scale_ref, store, gw)


def _pool_prompt(u, pool_mix, pool_scale, tm=512):
    n, pw = u.shape
    ng, gw, _ = pool_mix.shape
    hb = tm // POOL_HIST
    return pl.pallas_call(
        functools.partial(_pool_prompt_kernel, tm=tm, gw=gw),
        grid=(n // tm,),
        in_specs=[pl.BlockSpec((POOL_HIST, pw), lambda i: (jnp.maximum(i * hb - 1, 0), 0)),
                  pl.BlockSpec((tm, pw), lambda i: (i, 0)),
                  pl.BlockSpec((ng, gw, gw), lambda i: (0, 0, 0)),
                  pl.BlockSpec((1, pw), lambda i: (0, 0))],
        out_specs=pl.BlockSpec((tm, pw), lambda i: (i, 0)),
        out_shape=jax.ShapeDtypeStruct((n, pw), BF16),
        scratch_shapes=[pltpu.VMEM((POOL_HIST + tm, pw), F32)],
        compiler_params=_cparams(1, 32),
        name="pool_prompt",
    )(u, u, pool_mix, pool_scale.reshape(1, pw))


def _pool_sample_kernel(hist_ref, u_ref, mix_ref, scale_ref, o_ref, ext_ref, *, nb, t, gw, past_len):
    ext_ref[:, 0:POOL_HIST, :] = hist_ref[...]
    ext_ref[:, POOL_HIST:POOL_HIST + t, :] = u_ref[...]
    pos = past_len + (lax.broadcasted_iota(jnp.int32, (nb * t, 1), 0) & (t - 1))

    def store(lo, val):
        o_ref[:, lo:lo + gw] = val.astype(o_ref.dtype)

    _pool_groups(lambda dlt, lo: ext_ref[:, pl.ds(POOL_HIST - dlt, t), lo:lo + gw].reshape(nb * t, gw),
                 lambda lo: u_ref[:, :, lo:lo + gw].reshape(nb * t, gw), pos, mix_ref, scale_ref,
                 store, gw)


def _pool_sample(u3, hist3, pool_mix, pool_scale, past_len):
    nb, t, pw = u3.shape
    ng, gw, _ = pool_mix.shape
    return pl.pallas_call(
        functools.partial(_pool_sample_kernel, nb=nb, t=t, gw=gw, past_len=past_len),
        grid=(1,),
        in_specs=[pl.BlockSpec((nb, POOL_HIST, pw), lambda i: (0, 0, 0)),
                  pl.BlockSpec((nb, t, pw), lambda i: (0, 0, 0)),
                  pl.BlockSpec((ng, gw, gw), lambda i: (0, 0, 0)),
                  pl.BlockSpec((1, pw), lambda i: (0, 0))],
        out_specs=pl.BlockSpec((nb * t, pw), lambda i: (0, 0)),
        out_shape=jax.ShapeDtypeStruct((nb * t, pw), BF16),
        scratch_shapes=[pltpu.VMEM((nb, POOL_HIST + t, pw), F32)],
        compiler_params=_cparams(1, 32),
        name="pool_sample",
    )(hist3, u3, pool_mix, pool_scale.reshape(1, pw))


def _softplus2(y):
    return jnp.maximum(y, 0.0) + jnp.log2(1.0 + jnp.exp2(-jnp.abs(y)))


def _suffix_matrix(nk, nrep):
    r = lax.broadcasted_iota(jnp.int32, (nk, nk + nrep), 0)
    c = lax.broadcasted_iota(jnp.int32, (nk, nk + nrep), 1)
    return jnp.where((r > c) | (c >= nk), 1.0, 0.0).astype(BF16)


def _attn_prompt_kernel(bias_ref, q_ref, k_ref, v_ref, o_ref, acc_ref, car_ref, ls_ref, r_ref,
                        *, tq, tk):
    h = pl.program_id(0)
    i = pl.program_id(1)
    bias = bias_ref[h]
    q = q_ref[...]
    nsub = tq // tk
    sufm = _suffix_matrix(tk, 0)
    row = lax.broadcasted_iota(jnp.int32, (tq, tk), 0)
    col = lax.broadcasted_iota(jnp.int32, (tq, tk), 1)

    def sub_blocks(j):
        for u in range(nsub):
            off = (nsub - 1 - u) * tk
            yield u, off, pl.multiple_of(j * tq + off, tk)

    def scores(j, masked, slot):
        for u, off, start in sub_blocks(j):
            y = lax.dot_general(q, k_ref[pl.ds(start, tk), :], (((1,), (1,)), ((), ())),
                                preferred_element_type=F32) + bias
            s2 = _softplus2(y)
            lk = jnp.where(col + off < row, s2, 0.0) if masked else s2
            r_ref[slot, u, :, 0:tk] = jnp.dot(lk.astype(BF16), sufm, preferred_element_type=F32)
            r_ref[slot, u, :, tk:tk + HEAD_DIM] = jnp.broadcast_to(
                jnp.sum(lk, axis=1, keepdims=True), (tq, HEAD_DIM))
            ls_ref[slot, u] = y - s2

    def weights(j, masked, slot):
        car = car_ref[...]
        acc = acc_ref[...]
        for u, off, start in sub_blocks(j):
            a = jnp.exp2(ls_ref[slot, u] - r_ref[slot, u, :, 0:tk]
                         - jnp.concatenate([car] * (tk // HEAD_DIM), axis=1))
            if masked:
                a = jnp.where(col + off < row, a, 0.0)
            acc = acc + jnp.dot(a.astype(BF16), v_ref[pl.ds(start, tk), :],
                                preferred_element_type=F32)
            car = car + r_ref[slot, u, :, tk:tk + HEAD_DIM]
        acc_ref[...] = acc
        car_ref[...] = car

    acc_ref[...] = jnp.zeros_like(acc_ref)
    car_ref[...] = jnp.zeros_like(car_ref)
    scores(i, True, 0)

    @pl.when(i == 0)
    def _():
        weights(0, True, 0)

    @pl.when(i > 0)
    def _():
        weights(i, True, 0)
        scores(i - 1, False, 1)

        def body(s, c):
            j = i - 1 - s
            slot = (s + 1) & 1
            weights(j, False, slot)
            scores(j - 1, False, 1 - slot)
            return c

        lax.fori_loop(0, i - 1, body, 0)
        weights(0, False, i & 1)

    o_ref[...] = acc_ref[...].astype(o_ref.dtype)


def _attn_prompt(qs, kb, vb, bias2, tq=ATTN_TQ, tk=ATTN_TK):
    s, aw = qs.shape
    nh = aw // HEAD_DIM
    nsub = tq // tk
    return pl.pallas_call(
        functools.partial(_attn_prompt_kernel, tq=tq, tk=tk),
        grid_spec=pltpu.PrefetchScalarGridSpec(
            num_scalar_prefetch=1,
            grid=(nh, s // tq),
            in_specs=[pl.BlockSpec((tq, HEAD_DIM), lambda h, i, b: (i, h)),
                      pl.BlockSpec((s, HEAD_DIM), lambda h, i, b: (0, h)),
                      pl.BlockSpec((s, HEAD_DIM), lambda h, i, b: (0, h))],
            out_specs=pl.BlockSpec((tq, HEAD_DIM), lambda h, i, b: (i, h)),
            scratch_shapes=[pltpu.VMEM((tq, HEAD_DIM), F32), pltpu.VMEM((tq, HEAD_DIM), F32),
                            pltpu.VMEM((2, nsub, tq, tk), F32),
                            pltpu.VMEM((2, nsub, tq, tk + HEAD_DIM), F32)]),
        out_shape=jax.ShapeDtypeStruct((s, aw), BF16),
        compiler_params=_cparams(2, 32),
        name="attn_prompt",
    )(bias2, qs, kb, vb)


def _attn_sample_kernel(pt_ref, bias_ref, q_ref, kn_ref, vn_ref, ck_ref, cv_ref, o_ref,
                        acc_ref, car_ref, kpad_ref, vpad_ref, kbuf, vbuf, sem,
                        *, pps, nh, t, skew, n_pages):
    b = pl.program_id(0)
    c = pl.program_id(1)
    nc = pl.num_programs(1)
    step = b * nc + c
    slot = step & 1
    ht = nh * t
    page = HEAD_DIM
    sufm = _suffix_matrix(page, HEAD_DIM)
    q = q_ref[...]
    head = lambda h: slice(h * HEAD_DIM, (h + 1) * HEAD_DIM)
    rows = lambda h: slice(h * t, (h + 1) * t)
    qall = jnp.concatenate([q[:, head(h)] for h in range(nh)], axis=0).astype(BF16)
    bias_rows = jnp.concatenate([jnp.full((t, page), bias_ref[h], F32) for h in range(nh)], axis=0)
    kpos = lax.broadcasted_iota(jnp.int32, (ht, page), 1)
    qpos = lax.broadcasted_iota(jnp.int32, (ht, page), 0) & (t - 1)

    def page_copies(bb, cc, sl):
        cps = []
        for i in range(pps):
            pg = pt_ref[bb, n_pages - 1 - (cc * pps + i)]
            for h in range(nh):
                cps.append(pltpu.make_async_copy(ck_ref.at[pg, :, h, :], kbuf.at[sl, i, h], sem.at[sl]))
                cps.append(pltpu.make_async_copy(cv_ref.at[pg, :, h, :], vbuf.at[sl, i, h], sem.at[sl]))
        return cps

    def scores(k3, masked):
        k2 = k3.reshape(nh * page, HEAD_DIM).astype(BF16)
        z = lax.dot_general(qall, k2, (((1,), (1,)), ((), ())), preferred_element_type=F32)
        y = jnp.concatenate([z[rows(h), h * page:(h + 1) * page] for h in range(nh)], axis=0) + bias_rows
        s2 = _softplus2(y)
        lk = jnp.where(kpos < qpos, s2, 0.0) if masked else s2
        return y - s2, jnp.dot(lk.astype(BF16), sufm, preferred_element_type=F32)

    def weights(v3, ls, r, car, acc, masked):
        a = jnp.exp2(ls - r[:, :page] - car)
        if masked:
            a = jnp.where(kpos < qpos, a, 0.0)
        zero = jnp.zeros((t, page), F32)
        a_bd = jnp.concatenate(
            [jnp.concatenate([a[rows(h)] if g == h else zero for g in range(nh)], axis=1)
             for h in range(nh)], axis=0).astype(BF16)
        v2 = v3.reshape(nh * page, HEAD_DIM).astype(BF16)
        return car + r[:, page:], acc + jnp.dot(a_bd, v2, preferred_element_type=F32)

    def process(pages, masked):
        car = car_ref[...]
        acc = acc_ref[...]
        staged = []
        for idx in range(len(pages) + skew):
            if idx < len(pages):
                staged.append(scores(pages[idx][0](), masked))
            if idx >= skew:
                car, acc = weights(pages[idx - skew][1](), *staged[idx - skew], car, acc, masked)
        car_ref[...] = car
        acc_ref[...] = acc

    def start_all(cps):
        for n, cp in enumerate(cps):
            cp.start(priority=n & 1)

    @pl.when(step == 0)
    def _():
        start_all(page_copies(0, 0, 0))

    @pl.when(step + 1 < pl.num_programs(0) * nc)
    def _():
        wrap = c + 1 == nc
        start_all(page_copies(jnp.where(wrap, b + 1, b), jnp.where(wrap, 0, c + 1), 1 - slot))

    for cp in page_copies(b, c, slot):
        cp.wait()

    @pl.when(c == 0)
    def _():
        acc_ref[...] = jnp.zeros_like(acc_ref)
        car_ref[...] = jnp.zeros_like(car_ref)
        kpad_ref[...] = jnp.zeros_like(kpad_ref)
        vpad_ref[...] = jnp.zeros_like(vpad_ref)
        for h in range(nh):
            kpad_ref[h, 0:t, :] = kn_ref[:, head(h)]
            vpad_ref[h, 0:t, :] = vn_ref[:, head(h)]
        process([(lambda: kpad_ref[...], lambda: vpad_ref[...])], True)

    process([(functools.partial(lambda i: kbuf[slot, i], i), functools.partial(lambda i: vbuf[slot, i], i))
             for i in range(pps)], False)

    @pl.when(c == nc - 1)
    def _():
        acc = acc_ref[...]
        for h in range(nh):
            o_ref[:, head(h)] = acc[rows(h)]


def _attn_sample(q, kn, vn, cache_k, cache_v, page_table, bias2, t, pps=PAGES_PER_STEP):
    nbt, aw = q.shape
    nb = nbt // t
    nh = aw // HEAD_DIM
    n_pages = page_table.shape[1]
    page = cache_k.shape[1]
    assert cache_k.shape[1:] == (HEAD_DIM, nh, HEAD_DIM) and n_pages % pps == 0
    row_spec = pl.BlockSpec((t, aw), lambda b, c, pt, bias: (b, 0))
    any_spec = pl.BlockSpec(memory_space=pl.ANY)
    return pl.pallas_call(
        functools.partial(_attn_sample_kernel, pps=pps, nh=nh, t=t, skew=PAGE_SKEW, n_pages=n_pages),
        grid_spec=pltpu.PrefetchScalarGridSpec(
            num_scalar_prefetch=2,
            grid=(nb, n_pages // pps),
            in_specs=[row_spec, row_spec, row_spec, any_spec, any_spec],
            out_specs=row_spec,
            scratch_shapes=[pltpu.VMEM((nh * t, HEAD_DIM), F32), pltpu.VMEM((nh * t, HEAD_DIM), F32),
                            pltpu.VMEM((nh, page, HEAD_DIM), F32),
                            pltpu.VMEM((nh, page, HEAD_DIM), F32),
                            pltpu.VMEM((2, pps, nh, page, HEAD_DIM), F32),
                            pltpu.VMEM((2, pps, nh, page, HEAD_DIM), F32),
                            pltpu.SemaphoreType.DMA((2,))]),
        out_shape=jax.ShapeDtypeStruct((nbt, aw), F32),
        compiler_params=_cparams(2, 40),
        name="attn_sample",
    )(page_table, bias2, q, kn, vn, cache_k, cache_v)


def _merge_kernel(po_ref, at_ref, gt_ref, x_ref, g1_ref, n2_ref, sc_ref, sh_ref,
                  wbp_ref, wba_ref, wo_ref, *refs, d, n_blocks):
    x1_ref, h2_ref = refs[-2:]
    i = pl.program_id(0)

    @pl.when(i < n_blocks)
    def _():
        bp = jnp.dot(po_ref[...], wbp_ref[...], preferred_element_type=F32)
        ba = jnp.dot(at_ref[...].astype(BF16), wba_ref[...], preferred_element_type=F32)
        merged = gt_ref[:, :d].astype(F32) * bp + gt_ref[:, d:].astype(F32) * ba
        x1 = x_ref[...] + g1_ref[...] * jnp.dot(merged.astype(BF16), wo_ref[...],
                                                preferred_element_type=F32)
        x1_ref[...] = x1
        h2_ref[...] = _rms(x1, n2_ref[...]) * (1.0 + sc_ref[...]) + sh_ref[...]

    @pl.when(i >= n_blocks)
    def _():
        h2_ref[...] = jnp.zeros_like(h2_ref)


def _merge(pool_out, attn, gates, x, gate1, norm2_g, scale2, shift2, wbp, wba, wo, tm,
           n_all, row0, h2_prev=None):
    n, d = x.shape
    pw, aw = pool_out.shape[1], attn.shape[1]
    rb0 = row0 // tm
    n_blocks = n // tm
    steps = n_blocks if h2_prev is not None else (n_all - row0) // tm
    blk = lambda i: jnp.minimum(i, n_blocks - 1)
    row = lambda w: pl.BlockSpec((tm, w), lambda i: (blk(i), 0))
    mod = lambda m: (pl.BlockSpec((1, d), lambda i: (0, 0)) if m.shape[0] == 1 else row(d))
    in_specs = [row(pw), row(aw), row(2 * d), row(d), mod(gate1),
                pl.BlockSpec((1, d), lambda i: (0, 0)), mod(scale2), mod(shift2),
                _resident(wbp.shape), _resident(wba.shape), _resident(wo.shape)]
    args = [pool_out, attn, gates, x, gate1, norm2_g.reshape(1, d), scale2, shift2, wbp, wba, wo]
    aliases = {}
    if h2_prev is not None:
        in_specs.append(pl.BlockSpec(memory_space=pl.ANY))
        args.append(h2_prev)
        aliases = {len(args) - 1: 1}
    return pl.pallas_call(
        functools.partial(_merge_kernel, d=d, n_blocks=n_blocks),
        grid=(steps,),
        in_specs=in_specs,
        out_specs=[row(d), pl.BlockSpec((tm, d), lambda i: (rb0 + i, 0))],
        out_shape=[jax.ShapeDtypeStruct((n, d), F32), jax.ShapeDtypeStruct((n_all, d), F32)],
        input_output_aliases=aliases,
        compiler_params=_cparams(1, 56),
        name="merge",
    )(*args)


def _first_max(v, ids, n):
    m = jnp.max(v, axis=0, keepdims=True)
    idx = jnp.min(jnp.where(v == m, ids, n), axis=0, keepdims=True)
    return m, idx


def _router_kernel(h_ref, wr_ref, rb_ref, ei_ref, wt_ref, *, ne, ng, tkg, topk):
    epg = ne // ng
    hb = h_ref[...].astype(BF16)
    logits = lax.dot_general(wr_ref[...], hb, (((1,), (1,)), ((), ())), preferred_element_type=F32)
    scores = jax.nn.sigmoid(logits)
    sel = scores + rb_ref[...]
    tm = sel.shape[1]
    neg = -jnp.inf
    fiota = lambda n: lax.broadcasted_iota(jnp.int32, (n, tm), 0).astype(F32)
    ids_g = fiota(epg)
    gs = []
    for g in range(ng):
        v = sel[g * epg:(g + 1) * epg, :]
        m1, i1 = _first_max(v, ids_g, float(epg))
        m2 = jnp.max(jnp.where(ids_g == i1, neg, v), axis=0, keepdims=True)
        gs.append(m1 + m2)
    gsc = jnp.concatenate(gs, axis=0)
    ids_n = fiota(ng)
    gmask = jnp.zeros((ng, tm), F32)
    for _ in range(tkg):
        _, gi = _first_max(gsc, ids_n, float(ng))
        hit = ids_n == gi
        gmask = jnp.where(hit, 1.0, gmask)
        gsc = jnp.where(hit, neg, gsc)
    emask = jnp.concatenate(
        [jnp.broadcast_to(gmask[g:g + 1, :], (epg, tm)) for g in range(ng)], axis=0)
    cand = jnp.where(emask > 0.5, sel, neg)
    ids_e = fiota(ne)
    eis, wts = [], []
    for _ in range(topk):
        _, ei = _first_max(cand, ids_e, float(ne))
        hit = ids_e == ei
        eis.append(ei)
        wts.append(jnp.sum(jnp.where(hit, scores, 0.0), axis=0, keepdims=True))
        cand = jnp.where(hit, neg, cand)
    w = jnp.concatenate(wts, axis=0)
    ei_ref[...] = jnp.concatenate(eis, axis=0).astype(jnp.int32)
    wt_ref[...] = w / jnp.sum(w, axis=0, keepdims=True) * ROUTED_SCALE


def _router(h2, wr_t, router_bias, tm):
    n, d = h2.shape
    ne = wr_t.shape[0]
    return pl.pallas_call(
        functools.partial(_router_kernel, ne=ne, ng=N_EXPERT_GROUPS, tkg=TOPK_GROUPS, topk=TOP_K),
        grid=(n // tm,),
        in_specs=[pl.BlockSpec((tm, d), lambda i: (i, 0)),
                  pl.BlockSpec((ne, d), lambda i: (0, 0)),
                  pl.BlockSpec((ne, 1), lambda i: (0, 0))],
        out_specs=[pl.BlockSpec((TOP_K, tm), lambda i: (0, i)),
                   pl.BlockSpec((TOP_K, tm), lambda i: (0, i))],
        out_shape=[jax.ShapeDtypeStruct((TOP_K, n), jnp.int32),
                   jax.ShapeDtypeStruct((TOP_K, n), F32)],
        compiler_params=_cparams(1, 32),
        name="router",
    )(h2, wr_t, router_bias.reshape(ne, 1))


def _expert_kernel(be_ref, nu_ref, tok_ref, tokn_ref, dstp_ref, dst_ref, h_ref, wg_ref, wu_ref,
                   wd_ref, y_ref, xbuf, ybuf, wgb, wub, wdb, gsem, ssem,
                   *, tm, unroll, pad_row0, n_chunk):
    b = pl.program_id(0)
    nu = nu_ref[0]
    slot = b & 1
    other = 1 - slot
    ed = wg_ref.shape[1]
    GATHER_PRIORITY, SCATTER_PRIORITY = 0, 1

    def gather(tref, s, r):
        return pltpu.make_async_copy(h_ref.at[pl.ds(tref[0, 0, r], 1)], xbuf.at[s, pl.ds(r, 1)],
                                     gsem.at[s])

    def scatter(row, s, r):
        return pltpu.make_async_copy(ybuf.at[s, pl.ds(r, 1)], y_ref.at[pl.ds(row, 1)], ssem.at[s])

    def wait_gather(s):
        pltpu.make_async_copy(h_ref.at[pl.ds(0, tm)], xbuf.at[s], gsem.at[s]).wait()

    def wait_scatter(s):
        pltpu.make_async_copy(ybuf.at[s], y_ref.at[pl.ds(0, tm)], ssem.at[s]).wait()

    def rolled(start_row):
        def body(i, c):
            for u in range(unroll):
                start_row(i * unroll + u)
            return c

        lax.fori_loop(0, tm // unroll, body, 0)

    @pl.when(b < nu)
    def _():
        @pl.when(b == 0)
        def _():
            ybuf[...] = jnp.zeros_like(ybuf)
            rolled(lambda r: gather(tok_ref, 0, r).start(GATHER_PRIORITY))
            rolled(lambda r: scatter(pad_row0 + r, 0, r).start(SCATTER_PRIORITY))

        @pl.when((b == 0) | (be_ref[b] != be_ref[jnp.maximum(b - 1, 0)]))
        def _():
            wgb[...] = wg_ref[...].astype(BF16)
            wub[...] = wu_ref[...].astype(BF16)
            wdb[...] = wd_ref[...].astype(BF16)

        wait_gather(slot)
        starts = ([functools.partial(lambda r: gather(tokn_ref, other, r).start(GATHER_PRIORITY), r)
                   for r in range(tm)]
                  + [functools.partial(
                      lambda r: scatter(dstp_ref[0, 0, r], other, r).start(SCATTER_PRIORITY), r)
                     for r in range(tm)])
        n_groups = 3 * n_chunk
        per = -(-len(starts) // n_groups)

        def issue_group(g):
            for st in starts[g * per:(g + 1) * per]:
                st()

        x = xbuf[slot].astype(BF16)
        cw = ed // n_chunk
        acts = []
        for c in range(n_chunk):
            cs = slice(c * cw, (c + 1) * cw)
            gate = jnp.dot(x, wgb[:, cs], preferred_element_type=F32)
            issue_group(2 * c)
            up = jnp.dot(x, wub[:, cs], preferred_element_type=F32)
            issue_group(2 * c + 1)
            acts.append((_silu(gate) * up).astype(BF16))
        y = None
        for c in range(n_chunk):
            part = jnp.dot(acts[c], wdb[c * cw:(c + 1) * cw, :], preferred_element_type=F32)
            y = part if y is None else y + part
            issue_group(2 * n_chunk + c)
        wait_scatter(slot)
        ybuf[slot] = y

        @pl.when(b == nu - 1)
        def _():
            wait_gather(other)
            rolled(lambda r: scatter(dst_ref[0, 0, r], slot, r).start(SCATTER_PRIORITY))
            wait_scatter(other)
            wait_scatter(slot)


def _experts(block_e, n_used, slot_tok, slot_dst, h2, w_gate, w_up, w_down, tm=MOE_TM):
    nb = block_e.shape[0]
    n_all, d = h2.shape
    ed = w_gate.shape[2]
    n_rows = n_all * TOP_K
    cur = pl.BlockSpec((1, 1, tm), lambda b, be, nu: (b, 0, 0), memory_space=pltpu.SMEM)
    nxt = pl.BlockSpec((1, 1, tm), lambda b, be, nu: (jnp.minimum(b + 1, nb - 1), 0, 0),
                       memory_space=pltpu.SMEM)
    any_spec = pl.BlockSpec(memory_space=pl.ANY)
    tok3 = slot_tok.reshape(nb, 1, tm)
    dst3 = slot_dst.reshape(nb, 1, tm)
    pad1 = (n_rows + tm + jnp.arange(tm, dtype=jnp.int32)).reshape(1, 1, tm)
    dstp3 = jnp.concatenate([pad1, dst3[:-1]], axis=0)
    return pl.pallas_call(
        functools.partial(_expert_kernel, tm=tm, unroll=MOE_DMA_UNROLL, pad_row0=n_rows,
                          n_chunk=MOE_N_CHUNK),
        grid_spec=pltpu.PrefetchScalarGridSpec(
            num_scalar_prefetch=2,
            grid=(nb,),
            in_specs=[cur, nxt, cur, cur, any_spec,
                      pl.BlockSpec((None, d, ed), lambda b, be, nu: (be[b], 0, 0)),
                      pl.BlockSpec((None, d, ed), lambda b, be, nu: (be[b], 0, 0)),
                      pl.BlockSpec((None, ed, d), lambda b, be, nu: (be[b], 0, 0))],
            out_specs=any_spec,
            scratch_shapes=[pltpu.VMEM((2, tm, d), F32), pltpu.VMEM((2, tm, d), F32),
                            pltpu.VMEM((d, ed), BF16), pltpu.VMEM((d, ed), BF16),
                            pltpu.VMEM((ed, d), BF16),
                            pltpu.SemaphoreType.DMA((2,)), pltpu.SemaphoreType.DMA((2,))]),
        out_shape=jax.ShapeDtypeStruct((n_rows + 2 * tm, d), F32),
        compiler_params=_cparams(1, 56),
        name="experts",
    )(block_e, n_used, tok3, tok3, dstp3, dst3, h2, w_gate, w_up, w_down)


def _dispatch_meta(eidx, n_experts, tm):
    n = eidx.shape[0]
    nk = n * TOP_K
    nb = -(-nk // tm) + n_experts
    flat_e = eidx.reshape(-1)
    order = jnp.argsort(flat_e, stable=True).astype(jnp.int32)
    counts = jnp.sum((flat_e[:, None] == jnp.arange(n_experts, dtype=jnp.int32)[None, :])
                     .astype(jnp.int32), axis=0)
    starts = jnp.cumsum(counts) - counts
    padded = (counts + tm - 1) // tm * tm
    pends = jnp.cumsum(padded)
    pstarts = pends - padded
    n_used = (pends[-1] // tm).astype(jnp.int32)
    blk = jnp.arange(nb, dtype=jnp.int32)
    blk_start = blk * tm
    be = jnp.minimum(jnp.sum((pends[None, :] <= blk_start[:, None]).astype(jnp.int32), axis=1),
                     n_experts - 1)
    be = jnp.where(blk < n_used, be, be[jnp.maximum(n_used - 1, 0)]).astype(jnp.int32)
    within = (blk_start - pstarts[be])[:, None] + jnp.arange(tm, dtype=jnp.int32)[None, :]
    valid = within < counts[be][:, None]
    flat = order[jnp.clip(starts[be][:, None] + within, 0, nk - 1)]
    tok = flat // TOP_K
    pad_row = nk + (blk & 1)[:, None] * tm + jnp.arange(tm, dtype=jnp.int32)[None, :]
    dst = jnp.where(valid, (flat % TOP_K) * n + tok, pad_row)
    return be, n_used.reshape(1), tok.astype(jnp.int32), dst.astype(jnp.int32)


def _combine_kernel(*refs, d, sd):
    y_refs = refs[:TOP_K]
    wt_ref, h_ref, x_ref, g2_ref, fg_ref, wgu_ref, wd_ref, o_ref = refs[TOP_K:]
    wt = wt_ref[...]
    routed = y_refs[0][...] * wt[:, 0:1]
    for k in range(1, TOP_K):
        routed = routed + y_refs[k][...] * wt[:, k:k + 1]
    hb = h_ref[...].astype(BF16)
    gu = jnp.dot(hb, wgu_ref[...], preferred_element_type=F32)
    act = (_silu(gu[:, :sd]) * gu[:, sd:]).astype(BF16)
    ffn = routed + jnp.dot(act, wd_ref[...], preferred_element_type=F32)
    x2 = x_ref[...] + g2_ref[...] * ffn
    o_ref[...] = _rms(x2, fg_ref[...])


def _combine(y, n_all, row0, wts, h2, x1, gate2, final_g, ws_gu, ws_d, tm):
    n, d = x1.shape
    sd = ws_d.shape[0]
    rb0 = row0 // tm
    kb = n_all // tm
    row = lambda w: pl.BlockSpec((tm, w), lambda i: (i, 0))
    shifted = lambda w: pl.BlockSpec((tm, w), lambda i: (rb0 + i, 0))
    y_specs = [pl.BlockSpec((tm, d), functools.partial(lambda i, k: (k * kb + rb0 + i, 0), k=k))
               for k in range(TOP_K)]
    return pl.pallas_call(
        functools.partial(_combine_kernel, d=d, sd=sd),
        grid=(n // tm,),
        in_specs=y_specs + [shifted(TOP_K), shifted(d), row(d), _mod_spec(gate2, tm, d),
                            pl.BlockSpec((1, d), lambda i: (0, 0)),
                            _resident(ws_gu.shape), _resident(ws_d.shape)],
        out_specs=row(d),
        out_shape=jax.ShapeDtypeStruct((n, d), F32),
        compiler_params=_cparams(1, 48),
        name="combine",
    )(*([y] * TOP_K), wts, h2, x1, gate2, final_g.reshape(1, d), ws_gu, ws_d)


def _in_proj(h, w_in, pw, aw, d, tm, q_dtype, want_bf16_kv):
    tn = 1024
    ident = lambda acc: (acc,)
    (u,) = _mm(h, w_in, 0, pw, [F32], ident, tm, tn, "proj_u")
    qscale = LOG2E / math.sqrt(HEAD_DIM)
    (q,) = _mm(h, w_in, pw, aw, [q_dtype], lambda acc: (acc * qscale,), tm, tn, "proj_q")
    kv_dt = [F32, BF16] if want_bf16_kv else [F32]
    both = (lambda acc: (acc, acc)) if want_bf16_kv else ident
    k = _mm(h, w_in, pw + aw, aw, kv_dt, both, tm, tn, "proj_k")
    v = _mm(h, w_in, pw + 2 * aw, aw, kv_dt, both, tm, tn, "proj_v")
    (gates,) = _mm(h, w_in, pw + 3 * aw, 2 * d, [BF16], lambda acc: (jax.nn.sigmoid(acc),),
                   tm, tn, "proj_gates")
    return u, q, k, v, gates


def kernel(x_prompt, x_sample, c_prompt, c_sample, state_pool, cache_k, cache_v, page_table,
           norm1_g, norm2_g, w_ada, b_ada, w_in, pool_mix, pool_scale, w_br_pool, w_br_attn,
           sb_bias, w_o, w_router, router_bias, w_e_gate, w_e_up, w_e_down, w_s_gate, w_s_up,
           w_s_down, final_g):
    depth = w_in.shape[0]
    nbp, seq, d = x_prompt.shape
    nbs, t, _ = x_sample.shape
    assert nbp == 1 and depth == 1 and t & (t - 1) == 0
    l = 0
    pw = pool_scale.shape[1]
    aw = w_br_attn.shape[1]
    nh = aw // HEAD_DIM
    n_experts = w_router.shape[2]
    n_pages, page = page_table.shape[1], cache_k.shape[2]
    past_len = n_pages * page
    ns = nbs * t
    n_all = seq + ns
    buf = POOL_HIST - 1

    xp = x_prompt.reshape(seq, d)
    xs = x_sample.reshape(ns, d)
    rows = nbp + nbs
    rpad = -(-rows // 8) * 8
    c_all = jnp.concatenate([c_prompt, c_sample, jnp.zeros((rpad - rows, d), F32)], axis=0)

    mod = _ada(c_all, w_ada[l], b_ada[l])
    mod_p = [mod[0:1, i * d:(i + 1) * d] for i in range(N_MOD)]
    mod_s = [jnp.repeat(mod[nbp:rows, i * d:(i + 1) * d], t, axis=0) for i in range(N_MOD)]
    wbp, wba, wo = (w.astype(BF16) for w in (w_br_pool[l], w_br_attn[l], w_o[l]))
    wr_t = w_router[l].T.astype(BF16)
    ws_gu = jnp.concatenate([w_s_gate[l], w_s_up[l]], axis=1).astype(BF16)
    ws_d = w_s_down[l].astype(BF16)
    bias2 = sb_bias[l] * LOG2E

    hp = _normmod(xp, norm1_g[l], mod_p[1], mod_p[0], 512)
    u_p, q_p, (k_p, kb_p), (v_p, vb_p), gates_p = _in_proj(hp, w_in[l], pw, aw, d, 1024, BF16, True)
    pool_p = _pool_prompt(u_p, pool_mix[l], pool_scale[l])
    attn_p = _attn_prompt(q_p, kb_p, vb_p, bias2)
    x1_p, h2 = _merge(pool_p, attn_p, gates_p, xp, mod_p[2], norm2_g[l], mod_p[4], mod_p[3],
                      wbp, wba, wo, 256, n_all, 0)

    hs = _normmod(xs, norm1_g[l], mod_s[1], mod_s[0], ns)
    u_s, q_s, (k_s,), (v_s,), gates_s = _in_proj(hs, w_in[l], pw, aw, d, ns, F32, False)
    hist_s = jnp.concatenate([jnp.zeros((nbs, 1, pw), F32), state_pool[l]], axis=1)
    pool_s = _pool_sample(u_s.reshape(nbs, t, pw), hist_s, pool_mix[l], pool_scale[l], past_len)
    attn_s = _attn_sample(q_s, k_s, v_s, cache_k[l], cache_v[l], page_table, bias2, t)
    x1_s, h2 = _merge(pool_s, attn_s, gates_s, xs, mod_s[2], norm2_g[l], mod_s[4], mod_s[3],
                      wbp, wba, wo, ns, n_all, seq, h2_prev=h2)

    ei, wt = _router(h2, wr_t, router_bias[l], 768)
    be, n_used, slot_tok, slot_dst = _dispatch_meta(ei.T, n_experts, MOE_TM)
    y = _experts(be, n_used, slot_tok, slot_dst, h2, w_e_gate[l], w_e_up[l], w_e_down[l])
    wts = wt.T
    yp = _combine(y, n_all, 0, wts, h2, x1_p, mod_p[5], final_g, ws_gu, ws_d, 128)
    ys = _combine(y, n_all, seq, wts, h2, x1_s, mod_s[5], final_g, ws_gu, ws_d, 128)

    heads = lambda a, b_: a.reshape(1, b_, -1, nh, HEAD_DIM)
    pool_state_s = jnp.concatenate([state_pool[l][:, t:], u_s.reshape(nbs, t, pw)], axis=1)
    return (yp.reshape(nbp, seq, d), ys.reshape(nbs, t, d),
            u_p[seq - buf:].reshape(1, nbp, buf, pw), heads(k_p, nbp), heads(v_p, nbp),
            pool_state_s[None], heads(k_s, nbs), heads(v_s, nbs))
```

```python
import functools
import math

import jax
import jax.numpy as jnp
from jax import lax
from jax.experimental import pallas as pl
from jax.experimental.pallas import tpu as pltpu

F32 = jnp.float32
BF16 = jnp.bfloat16

HEAD_DIM = 128
POOL_WINDOWS = (2, 4, 8, 16)
POOL_HIST = 16
N_EXPERT_GROUPS = 8
TOPK_GROUPS = 4
TOP_K = 8
ROUTED_SCALE = 2.5
RMS_EPS = 1e-6
N_MOD = 6
LOG2E = 1.4426950408889634

MOE_TM = 256
MOE_DMA_UNROLL = 8
MOE_N_CHUNK = 2
ATTN_TQ = 512
ATTN_TK = 256
PAGES_PER_STEP = 16
PAGE_SKEW = 2
PAGE_HEAD_GROUP = 2
MIB = 1024 * 1024


def _cparams(n_axes, vmem_mib):
    return pltpu.CompilerParams(dimension_semantics=("arbitrary",) * n_axes,
                                vmem_limit_bytes=vmem_mib * MIB)


def _silu(x):
    return x * jax.nn.sigmoid(x)


def _resident(shape):
    return pl.BlockSpec(shape, lambda *_: (0,) * len(shape), pipeline_mode=pl.Buffered(1))


def _ada_kernel(c_ref, w_ref, b_ref, o_ref):
    s = _silu(c_ref[...]).astype(BF16)
    o_ref[...] = jnp.dot(s, w_ref[...].astype(BF16), preferred_element_type=F32) + b_ref[...]


def _ada(c_all, w_ada, b_ada, tn=512):
    r, d = c_all.shape
    n = w_ada.shape[1]
    return pl.pallas_call(
        _ada_kernel,
        grid=(n // tn,),
        in_specs=[pl.BlockSpec((r, d), lambda j: (0, 0)),
                  pl.BlockSpec((d, tn), lambda j: (0, j)),
                  pl.BlockSpec((1, tn), lambda j: (0, j))],
        out_specs=pl.BlockSpec((r, tn), lambda j: (0, j)),
        out_shape=jax.ShapeDtypeStruct((r, n), F32),
        compiler_params=_cparams(1, 32),
        name="ada_mod",
    )(c_all, w_ada, b_ada.reshape(1, n))


def _rms(x, g):
    return x * lax.rsqrt(jnp.mean(x * x, axis=-1, keepdims=True) + RMS_EPS) * g


def _normmod_kernel(x_ref, g_ref, sc_ref, sh_ref, o_ref):
    xn = _rms(x_ref[...], g_ref[...])
    o_ref[...] = (xn * (1.0 + sc_ref[...]) + sh_ref[...]).astype(o_ref.dtype)


def _mod_spec(mod, tm, d):
    if mod.shape[0] == 1:
        return pl.BlockSpec((1, d), lambda i: (0, 0))
    return pl.BlockSpec((tm, d), lambda i: (i, 0))


def _normmod(x, g, scale, shift, tm):
    n, d = x.shape
    return pl.pallas_call(
        _normmod_kernel,
        grid=(n // tm,),
        in_specs=[pl.BlockSpec((tm, d), lambda i: (i, 0)),
                  pl.BlockSpec((1, d), lambda i: (0, 0)),
                  _mod_spec(scale, tm, d), _mod_spec(shift, tm, d)],
        out_specs=pl.BlockSpec((tm, d), lambda i: (i, 0)),
        out_shape=jax.ShapeDtypeStruct((n, d), BF16),
        compiler_params=_cparams(1, 32),
        name="norm_mod",
    )(x, g.reshape(1, d), scale, shift)


def _mm_kernel(a_ref, w_ref, *refs, epilogue, n_out):
    out_refs, wb_ref = refs[:n_out], refs[n_out]

    @pl.when(pl.program_id(1) == 0)
    def _():
        wb_ref[...] = w_ref[...].astype(BF16)

    acc = jnp.dot(a_ref[...], wb_ref[...], preferred_element_type=F32)
    for r, v in zip(out_refs, epilogue(acc)):
        r[...] = v.astype(r.dtype)


def _mm(a, w, col0, ncols, out_dtypes, epilogue, tm, tn, name):
    m, k = a.shape
    cb0 = col0 // tn
    out_spec = pl.BlockSpec((tm, tn), lambda j, i: (i, j))
    return pl.pallas_call(
        functools.partial(_mm_kernel, epilogue=epilogue, n_out=len(out_dtypes)),
        grid=(ncols // tn, m // tm),
        in_specs=[pl.BlockSpec((tm, k), lambda j, i: (i, 0)),
                  pl.BlockSpec((k, tn), lambda j, i: (0, cb0 + j))],
        out_specs=[out_spec] * len(out_dtypes),
        out_shape=[jax.ShapeDtypeStruct((m, ncols), dt) for dt in out_dtypes],
        scratch_shapes=[pltpu.VMEM((k, tn), BF16)],
        compiler_params=_cparams(2, 56),
        name=name,
    )(a, w)


def _pool_groups(win_fn, u_fn, pos, mix_ref, scale_ref, store_fn, gw):
    for g, w in enumerate(POOL_WINDOWS):
        lo = g * gw
        win = win_fn(0, lo)
        for dlt in range(1, w):
            win = win + win_fn(dlt, lo)
        cnt = jnp.minimum(pos + 1, w).astype(F32)
        dd = win / cnt - u_fn(lo)
        mixed = jnp.dot(dd.astype(BF16), mix_ref[g].astype(BF16), preferred_element_type=F32)
        store_fn(lo, mixed * scale_ref[:, lo:lo + gw])


def _pool_prompt_kernel(hist_ref, u_ref, mix_ref, scale_ref, o_ref, ext_ref, *, tm, gw):
    i = pl.program_id(0)
    ext_ref[0:POOL_HIST, :] = jnp.where(i == 0, 0.0, hist_ref[...])
    ext_ref[POOL_HIST:POOL_HIST + tm, :] = u_ref[...]
    pos = i * tm + lax.broadcasted_iota(jnp.int32, (tm, 1), 0)

    def store(lo, val):
        o_ref[:, lo:lo + gw] = val.astype(o_ref.dtype)

    _pool_groups(lambda dlt, lo: ext_ref[pl.ds(POOL_HIST - dlt, tm), lo:lo + gw],
                 lambda lo: u_ref[:, lo:lo + gw], pos, mix_ref, scale_ref, store, gw)


def _pool_prompt(u, pool_mix, pool_scale, tm=512):
    n, pw = u.shape
    ng, gw, _ = pool_mix.shape
    hb = tm // POOL_HIST
    return pl.pallas_call(
        functools.partial(_pool_prompt_kernel, tm=tm, gw=gw),
        grid=(n // tm,),
        in_specs=[pl.BlockSpec((POOL_HIST, pw), lambda i: (jnp.maximum(i * hb - 1, 0), 0)),
                  pl.BlockSpec((tm, pw), lambda i: (i, 0)),
                  pl.BlockSpec((ng, gw, gw), lambda i: (0, 0, 0)),
                  pl.BlockSpec((1, pw), lambda i: (0, 0))],
        out_specs=pl.BlockSpec((tm, pw), lambda i: (i, 0)),
        out_shape=jax.ShapeDtypeStruct((n, pw), BF16),
        scratch_shapes=[pltpu.VMEM((POOL_HIST + tm, pw), F32)],
        compiler_params=_cparams(1, 32),
        name="pool_prompt",
    )(u, u, pool_mix, pool_scale.reshape(1, pw))


def _pool_sample_kernel(hist_ref, u_ref, mix_ref, scale_ref, o_ref, ext_ref, *, nb, t, gw, past_len):
    ext_ref[:, 0:POOL_HIST, :] = hist_ref[...]
    ext_ref[:, POOL_HIST:POOL_HIST + t, :] = u_ref[...]
    pos = past_len + (lax.broadcasted_iota(jnp.int32, (nb * t, 1), 0) & (t - 1))

    def store(lo, val):
        o_ref[:, lo:lo + gw] = val.astype(o_ref.dtype)

    _pool_groups(lambda dlt, lo: ext_ref[:, pl.ds(POOL_HIST - dlt, t), lo:lo + gw].reshape(nb * t, gw),
                 lambda lo: u_ref[:, :, lo:lo + gw].reshape(nb * t, gw), pos, mix_ref, scale_ref,
                 store, gw)


def _pool_sample(u3, hist3, pool_mix, pool_scale, past_len):
    nb, t, pw = u3.shape
    ng, gw, _ = pool_mix.shape
    return pl.pallas_call(
        functools.partial(_pool_sample_kernel, nb=nb, t=t, gw=gw, past_len=past_len),
        grid=(1,),
        in_specs=[pl.BlockSpec((nb, POOL_HIST, pw), lambda i: (0, 0, 0)),
                  pl.BlockSpec((nb, t, pw), lambda i: (0, 0, 0)),
                  pl.BlockSpec((ng, gw, gw), lambda i: (0, 0, 0)),
                  pl.BlockSpec((1, pw), lambda i: (0, 0))],
        out_specs=pl.BlockSpec((nb * t, pw), lambda i: (0, 0)),
        out_shape=jax.ShapeDtypeStruct((nb * t, pw), BF16),
        scratch_shapes=[pltpu.VMEM((nb, POOL_HIST + t, pw), F32)],
        compiler_params=_cparams(1, 32),
        name="pool_sample",
    )(hist3, u3, pool_mix, pool_scale.reshape(1, pw))


def _softplus2(y):
    return jnp.maximum(y, 0.0) + jnp.log2(1.0 + jnp.exp2(-jnp.abs(y)))


def _suffix_matrix(nk, nrep):
    r = lax.broadcasted_iota(jnp.int32, (nk, nk + nrep), 0)
    c = lax.broadcasted_iota(jnp.int32, (nk, nk + nrep), 1)
    return jnp.where((r > c) | (c >= nk), 1.0, 0.0).astype(BF16)


def _attn_prompt_kernel(bias_ref, q_ref, k_ref, v_ref, o_ref, acc_ref, car_ref, ls_ref, r_ref,
                        *, tq, tk):
    h = pl.program_id(0)
    i = pl.program_id(1)
    bias = bias_ref[h]
    q = q_ref[...]
    nsub = tq // tk
    sufm = _suffix_matrix(tk, 0)
    row = lax.broadcasted_iota(jnp.int32, (tq, tk), 0)
    col = lax.broadcasted_iota(jnp.int32, (tq, tk), 1)

    def sub_blocks(j):
        for u in range(nsub):
            off = (nsub - 1 - u) * tk
            yield u, off, pl.multiple_of(j * tq + off, tk)

    def scores(j, masked, slot):
        for u, off, start in sub_blocks(j):
            y = lax.dot_general(q, k_ref[pl.ds(start, tk), :], (((1,), (1,)), ((), ())),
                                preferred_element_type=F32) + bias
            s2 = _softplus2(y)
            lk = jnp.where(col + off < row, s2, 0.0) if masked else s2
            r_ref[slot, u, :, 0:tk] = jnp.dot(lk.astype(BF16), sufm, preferred_element_type=F32)
            r_ref[slot, u, :, tk:tk + HEAD_DIM] = jnp.broadcast_to(
                jnp.sum(lk, axis=1, keepdims=True), (tq, HEAD_DIM))
            ls_ref[slot, u] = y - s2

    def weights(j, masked, slot):
        car = car_ref[...]
        acc = acc_ref[...]
        for u, off, start in sub_blocks(j):
            a = jnp.exp2(ls_ref[slot, u] - r_ref[slot, u, :, 0:tk]
                         - jnp.concatenate([car] * (tk // HEAD_DIM), axis=1))
            if masked:
                a = jnp.where(col + off < row, a, 0.0)
            acc = acc + jnp.dot(a.astype(BF16), v_ref[pl.ds(start, tk), :],
                                preferred_element_type=F32)
            car = car + r_ref[slot, u, :, tk:tk + HEAD_DIM]
        acc_ref[...] = acc
        car_ref[...] = car

    acc_ref[...] = jnp.zeros_like(acc_ref)
    car_ref[...] = jnp.zeros_like(car_ref)
    scores(i, True, 0)

    @pl.when(i == 0)
    def _():
        weights(0, True, 0)

    @pl.when(i > 0)
    def _():
        weights(i, True, 0)
        scores(i - 1, False, 1)

        def body(s, c):
            j = i - 1 - s
            slot = (s + 1) & 1
            weights(j, False, slot)
            scores(j - 1, False, 1 - slot)
            return c

        lax.fori_loop(0, i - 1, body, 0)
        weights(0, False, i & 1)

    o_ref[...] = acc_ref[...].astype(o_ref.dtype)


def _attn_prompt(qs, kb, vb, bias2, tq=ATTN_TQ, tk=ATTN_TK):
    s, aw = qs.shape
    nh = aw // HEAD_DIM
    nsub = tq // tk
    return pl.pallas_call(
        functools.partial(_attn_prompt_kernel, tq=tq, tk=tk),
        grid_spec=pltpu.PrefetchScalarGridSpec(
            num_scalar_prefetch=1,
            grid=(nh, s // tq),
            in_specs=[pl.BlockSpec((tq, HEAD_DIM), lambda h, i, b: (i, h)),
                      pl.BlockSpec((s, HEAD_DIM), lambda h, i, b: (0, h)),
                      pl.BlockSpec((s, HEAD_DIM), lambda h, i, b: (0, h))],
            out_specs=pl.BlockSpec((tq, HEAD_DIM), lambda h, i, b: (i, h)),
            scratch_shapes=[pltpu.VMEM((tq, HEAD_DIM), F32), pltpu.VMEM((tq, HEAD_DIM), F32),
                            pltpu.VMEM((2, nsub, tq, tk), F32),
                            pltpu.VMEM((2, nsub, tq, tk + HEAD_DIM), F32)]),
        out_shape=jax.ShapeDtypeStruct((s, aw), BF16),
        compiler_params=_cparams(2, 32),
        name="attn_prompt",
    )(bias2, qs, kb, vb)


def _attn_sample_kernel(pt_ref, bias_ref, q_ref, kn_ref, vn_ref, ck_ref, cv_ref, o_ref,
                        acc_ref, car_ref, kpad_ref, vpad_ref, kbuf, vbuf, sem,
                        *, pps, nh, t, skew, n_pages, hg):
    b = pl.program_id(0)
    c = pl.program_id(1)
    nc = pl.num_programs(1)
    step = b * nc + c
    slot = step & 1
    ht = nh * t
    page = HEAD_DIM
    sufm = _suffix_matrix(page, HEAD_DIM)
    q = q_ref[...]
    head = lambda h: slice(h * HEAD_DIM, (h + 1) * HEAD_DIM)
    rows = lambda h: slice(h * t, (h + 1) * t)
    qall = jnp.concatenate([q[:, head(h)] for h in range(nh)], axis=0).astype(BF16)
    bias_rows = jnp.concatenate([jnp.full((t, page), bias_ref[h], F32) for h in range(nh)], axis=0)
    kpos = lax.broadcasted_iota(jnp.int32, (ht, page), 1)
    qpos = lax.broadcasted_iota(jnp.int32, (ht, page), 0) & (t - 1)

    def page_copies(bb, cc, sl):
        cps = []
        for i in range(pps):
            pg = pt_ref[bb, n_pages - 1 - (cc * pps + i)]
            for h in range(nh):
                cps.append(pltpu.make_async_copy(ck_ref.at[pg, :, h, :], kbuf.at[sl, i, h], sem.at[sl]))
                cps.append(pltpu.make_async_copy(cv_ref.at[pg, :, h, :], vbuf.at[sl, i, h], sem.at[sl]))
        return cps

    def scores(k3, masked):
        k2 = k3.reshape(nh * page, HEAD_DIM).astype(BF16)
        ys = []
        for g in range(nh // hg):
            z = lax.dot_general(qall[g * hg * t:(g + 1) * hg * t], k2[g * hg * page:(g + 1) * hg * page],
                                (((1,), (1,)), ((), ())), preferred_element_type=F32)
            ys += [z[rows(h), h * page:(h + 1) * page] for h in range(hg)]
        y = jnp.concatenate(ys, axis=0) + bias_rows
        s2 = _softplus2(y)
        lk = jnp.where(kpos < qpos, s2, 0.0) if masked else s2
        return y - s2, jnp.dot(lk.astype(BF16), sufm, preferred_element_type=F32)

    def weights(v3, ls, r, car, acc, masked):
        a = jnp.exp2(ls - r[:, :page] - car)
        if masked:
            a = jnp.where(kpos < qpos, a, 0.0)
        zero = jnp.zeros((t, page), F32)
        v2 = v3.reshape(nh * page, HEAD_DIM).astype(BF16)
        outs = []
        for g in range(nh // hg):
            a_bd = jnp.concatenate(
                [jnp.concatenate([a[rows(g * hg + h)] if gg == h else zero for gg in range(hg)], axis=1)
                 for h in range(hg)], axis=0).astype(BF16)
            outs.append(jnp.dot(a_bd, v2[g * hg * page:(g + 1) * hg * page], preferred_element_type=F32))
        return car + r[:, page:], acc + jnp.concatenate(outs, axis=0)

    def process(pages, masked):
        car = car_ref[...]
        acc = acc_ref[...]
        staged = []
        for idx in range(len(pages) + skew):
            if idx < len(pages):
                staged.append(scores(pages[idx][0](), masked))
            if idx >= skew:
                car, acc = weights(pages[idx - skew][1](), *staged[idx - skew], car, acc, masked)
        car_ref[...] = car
        acc_ref[...] = acc

    def start_all(cps):
        for n, cp in enumerate(cps):
            cp.start(priority=n & 1)

    @pl.when(step == 0)
    def _():
        start_all(page_copies(0, 0, 0))

    @pl.when(step + 1 < pl.num_programs(0) * nc)
    def _():
        wrap = c + 1 == nc
        start_all(page_copies(jnp.where(wrap, b + 1, b), jnp.where(wrap, 0, c + 1), 1 - slot))

    for cp in page_copies(b, c, slot):
        cp.wait()

    @pl.when(c == 0)
    def _():
        acc_ref[...] = jnp.zeros_like(acc_ref)
        car_ref[...] = jnp.zeros_like(car_ref)
        kpad_ref[...] = jnp.zeros_like(kpad_ref)
        vpad_ref[...] = jnp.zeros_like(vpad_ref)
        for h in range(nh):
            kpad_ref[h, 0:t, :] = kn_ref[:, head(h)]
            vpad_ref[h, 0:t, :] = vn_ref[:, head(h)]
        process([(lambda: kpad_ref[...], lambda: vpad_ref[...])], True)

    process([(functools.partial(lambda i: kbuf[slot, i], i), functools.partial(lambda i: vbuf[slot, i], i))
             for i in range(pps)], False)

    @pl.when(c == nc - 1)
    def _():
        acc = acc_ref[...]
        for h in range(nh):
            o_ref[:, head(h)] = acc[rows(h)]


def _attn_sample(q, kn, vn, cache_k, cache_v, page_table, bias2, t, pps=PAGES_PER_STEP):
    nbt, aw = q.shape
    nb = nbt // t
    nh = aw // HEAD_DIM
    n_pages = page_table.shape[1]
    page = cache_k.shape[1]
    assert cache_k.shape[1:] == (HEAD_DIM, nh, HEAD_DIM) and n_pages % pps == 0
    row_spec = pl.BlockSpec((t, aw), lambda b, c, pt, bias: (b, 0))
    any_spec = pl.BlockSpec(memory_space=pl.ANY)
    return pl.pallas_call(
        functools.partial(_attn_sample_kernel, pps=pps, nh=nh, t=t, skew=PAGE_SKEW, n_pages=n_pages,
                          hg=PAGE_HEAD_GROUP),
        grid_spec=pltpu.PrefetchScalarGridSpec(
            num_scalar_prefetch=2,
            grid=(nb, n_pages // pps),
            in_specs=[row_spec, row_spec, row_spec, any_spec, any_spec],
            out_specs=row_spec,
            scratch_shapes=[pltpu.VMEM((nh * t, HEAD_DIM), F32), pltpu.VMEM((nh * t, HEAD_DIM), F32),
                            pltpu.VMEM((nh, page, HEAD_DIM), F32),
                            pltpu.VMEM((nh, page, HEAD_DIM), F32),
                            pltpu.VMEM((2, pps, nh, page, HEAD_DIM), F32),
                            pltpu.VMEM((2, pps, nh, page, HEAD_DIM), F32),
                            pltpu.SemaphoreType.DMA((2,))]),
        out_shape=jax.ShapeDtypeStruct((nbt, aw), F32),
        compiler_params=_cparams(2, 56),
        name="attn_sample",
    )(page_table, bias2, q, kn, vn, cache_k, cache_v)


def _merge_kernel(po_ref, at_ref, gt_ref, x_ref, g1_ref, n2_ref, sc_ref, sh_ref,
                  wbp_ref, wba_ref, wo_ref, *refs, d, n_blocks):
    x1_ref, h2_ref = refs[-2:]
    i = pl.program_id(0)

    @pl.when(i < n_blocks)
    def _():
        bp = jnp.dot(po_ref[...], wbp_ref[...], preferred_element_type=F32)
        ba = jnp.dot(at_ref[...].astype(BF16), wba_ref[...], preferred_element_type=F32)
        merged = gt_ref[:, :d].astype(F32) * bp + gt_ref[:, d:].astype(F32) * ba
        x1 = x_ref[...] + g1_ref[...] * jnp.dot(merged.astype(BF16), wo_ref[...],
                                                preferred_element_type=F32)
        x1_ref[...] = x1
        h2_ref[...] = _rms(x1, n2_ref[...]) * (1.0 + sc_ref[...]) + sh_ref[...]

    @pl.when(i >= n_blocks)
    def _():
        h2_ref[...] = jnp.zeros_like(h2_ref)


def _merge(pool_out, attn, gates, x, gate1, norm2_g, scale2, shift2, wbp, wba, wo, tm,
           n_all, row0, h2_prev=None):
    n, d = x.shape
    pw, aw = pool_out.shape[1], attn.shape[1]
    rb0 = row0 // tm
    n_blocks = n // tm
    steps = n_blocks if h2_prev is not None else (n_all - row0) // tm
    blk = lambda i: jnp.minimum(i, n_blocks - 1)
    row = lambda w: pl.BlockSpec((tm, w), lambda i: (blk(i), 0))
    mod = lambda m: (pl.BlockSpec((1, d), lambda i: (0, 0)) if m.shape[0] == 1 else row(d))
    in_specs = [row(pw), row(aw), row(2 * d), row(d), mod(gate1),
                pl.BlockSpec((1, d), lambda i: (0, 0)), mod(scale2), mod(shift2),
                _resident(wbp.shape), _resident(wba.shape), _resident(wo.shape)]
    args = [pool_out, attn, gates, x, gate1, norm2_g.reshape(1, d), scale2, shift2, wbp, wba, wo]
    aliases = {}
    if h2_prev is not None:
        in_specs.append(pl.BlockSpec(memory_space=pl.ANY))
        args.append(h2_prev)
        aliases = {len(args) - 1: 1}
    return pl.pallas_call(
        functools.partial(_merge_kernel, d=d, n_blocks=n_blocks),
        grid=(steps,),
        in_specs=in_specs,
        out_specs=[row(d), pl.BlockSpec((tm, d), lambda i: (rb0 + i, 0))],
        out_shape=[jax.ShapeDtypeStruct((n, d), F32), jax.ShapeDtypeStruct((n_all, d), F32)],
        input_output_aliases=aliases,
        compiler_params=_cparams(1, 56),
        name="merge",
    )(*args)


def _first_max(v, ids, n):
    m = jnp.max(v, axis=0, keepdims=True)
    idx = jnp.min(jnp.where(v == m, ids, n), axis=0, keepdims=True)
    return m, idx


def _router_kernel(h_ref, wr_ref, rb_ref, ei_ref, wt_ref, *, ne, ng, tkg, topk):
    epg = ne // ng
    hb = h_ref[...].astype(BF16)
    logits = lax.dot_general(wr_ref[...], hb, (((1,), (1,)), ((), ())), preferred_element_type=F32)
    scores = jax.nn.sigmoid(logits)
    sel = scores + rb_ref[...]
    tm = sel.shape[1]
    neg = -jnp.inf
    fiota = lambda n: lax.broadcasted_iota(jnp.int32, (n, tm), 0).astype(F32)
    ids_g = fiota(epg)
    gs = []
    for g in range(ng):
        v = sel[g * epg:(g + 1) * epg, :]
        m1, i1 = _first_max(v, ids_g, float(epg))
        m2 = jnp.max(jnp.where(ids_g == i1, neg, v), axis=0, keepdims=True)
        gs.append(m1 + m2)
    gsc = jnp.concatenate(gs, axis=0)
    ids_n = fiota(ng)
    gmask = jnp.zeros((ng, tm), F32)
    for _ in range(tkg):
        _, gi = _first_max(gsc, ids_n, float(ng))
        hit = ids_n == gi
        gmask = jnp.where(hit, 1.0, gmask)
        gsc = jnp.where(hit, neg, gsc)
    emask = jnp.concatenate(
        [jnp.broadcast_to(gmask[g:g + 1, :], (epg, tm)) for g in range(ng)], axis=0)
    cand = jnp.where(emask > 0.5, sel, neg)
    ids_e = fiota(ne)
    eis, wts = [], []
    for _ in range(topk):
        _, ei = _first_max(cand, ids_e, float(ne))
        hit = ids_e == ei
        eis.append(ei)
        wts.append(jnp.sum(jnp.where(hit, scores, 0.0), axis=0, keepdims=True))
        cand = jnp.where(hit, neg, cand)
    w = jnp.concatenate(wts, axis=0)
    ei_ref[...] = jnp.concatenate(eis, axis=0).astype(jnp.int32)
    wt_ref[...] = w / jnp.sum(w, axis=0, keepdims=True) * ROUTED_SCALE


def _router(h2, wr_t, router_bias, tm):
    n, d = h2.shape
    ne = wr_t.shape[0]
    return pl.pallas_call(
        functools.partial(_router_kernel, ne=ne, ng=N_EXPERT_GROUPS, tkg=TOPK_GROUPS, topk=TOP_K),
        grid=(n // tm,),
        in_specs=[pl.BlockSpec((tm, d), lambda i: (i, 0)),
                  pl.BlockSpec((ne, d), lambda i: (0, 0)),
                  pl.BlockSpec((ne, 1), lambda i: (0, 0))],
        out_specs=[pl.BlockSpec((TOP_K, tm), lambda i: (0, i)),
                   pl.BlockSpec((TOP_K, tm), lambda i: (0, i))],
        out_shape=[jax.ShapeDtypeStruct((TOP_K, n), jnp.int32),
                   jax.ShapeDtypeStruct((TOP_K, n), F32)],
        compiler_params=_cparams(1, 32),
        name="router",
    )(h2, wr_t, router_bias.reshape(ne, 1))


def _expert_kernel(be_ref, nu_ref, tok_ref, tokn_ref, dstp_ref, dst_ref, h_ref, wg_ref, wu_ref,
                   wd_ref, y_ref, xbuf, ybuf, wgb, wub, wdb, gsem, ssem,
                   *, tm, unroll, pad_row0, n_chunk):
    b = pl.program_id(0)
    nu = nu_ref[0]
    slot = b & 1
    other = 1 - slot
    ed = wg_ref.shape[1]
    GATHER_PRIORITY, SCATTER_PRIORITY = 0, 1

    def gather(tref, s, r):
        return pltpu.make_async_copy(h_ref.at[pl.ds(tref[0, 0, r], 1)], xbuf.at[s, pl.ds(r, 1)],
                                     gsem.at[s])

    def scatter(row, s, r):
        return pltpu.make_async_copy(ybuf.at[s, pl.ds(r, 1)], y_ref.at[pl.ds(row, 1)], ssem.at[s])

    def wait_gather(s):
        pltpu.make_async_copy(h_ref.at[pl.ds(0, tm)], xbuf.at[s], gsem.at[s]).wait()

    def wait_scatter(s):
        pltpu.make_async_copy(ybuf.at[s], y_ref.at[pl.ds(0, tm)], ssem.at[s]).wait()

    def rolled(start_row):
        def body(i, c):
            for u in range(unroll):
                start_row(i * unroll + u)
            return c

        lax.fori_loop(0, tm // unroll, body, 0)

    @pl.when(b < nu)
    def _():
        @pl.when(b == 0)
        def _():
            ybuf[...] = jnp.zeros_like(ybuf)
            rolled(lambda r: gather(tok_ref, 0, r).start(GATHER_PRIORITY))
            rolled(lambda r: scatter(pad_row0 + r, 0, r).start(SCATTER_PRIORITY))

        @pl.when((b == 0) | (be_ref[b] != be_ref[jnp.maximum(b - 1, 0)]))
        def _():
            wgb[...] = wg_ref[...].astype(BF16)
            wub[...] = wu_ref[...].astype(BF16)
            wdb[...] = wd_ref[...].astype(BF16)

        wait_gather(slot)
        starts = ([functools.partial(lambda r: gather(tokn_ref, other, r).start(GATHER_PRIORITY), r)
                   for r in range(tm)]
                  + [functools.partial(
                      lambda r: scatter(dstp_ref[0, 0, r], other, r).start(SCATTER_PRIORITY), r)
                     for r in range(tm)])
        n_groups = 3 * n_chunk
        per = -(-len(starts) // n_groups)

        def issue_group(g):
            for st in starts[g * per:(g + 1) * per]:
                st()

        x = xbuf[slot].astype(BF16)
        cw = ed // n_chunk
        acts = []
        for c in range(n_chunk):
            cs = slice(c * cw, (c + 1) * cw)
            gate = jnp.dot(x, wgb[:, cs], preferred_element_type=F32)
            issue_group(2 * c)
            up = jnp.dot(x, wub[:, cs], preferred_element_type=F32)
            issue_group(2 * c + 1)
            acts.append((_silu(gate) * up).astype(BF16))
        y = None
        for c in range(n_chunk):
            part = jnp.dot(acts[c], wdb[c * cw:(c + 1) * cw, :], preferred_element_type=F32)
            y = part if y is None else y + part
            issue_group(2 * n_chunk + c)
        wait_scatter(slot)
        ybuf[slot] = y

        @pl.when(b == nu - 1)
        def _():
            wait_gather(other)
            rolled(lambda r: scatter(dst_ref[0, 0, r], slot, r).start(SCATTER_PRIORITY))
            wait_scatter(other)
            wait_scatter(slot)


def _experts(block_e, n_used, slot_tok, slot_dst, h2, w_gate, w_up, w_down, tm=MOE_TM):
    nb = block_e.shape[0]
    n_all, d = h2.shape
    ed = w_gate.shape[2]
    n_rows = n_all * TOP_K
    cur = pl.BlockSpec((1, 1, tm), lambda b, be, nu: (b, 0, 0), memory_space=pltpu.SMEM)
    nxt = pl.BlockSpec((1, 1, tm), lambda b, be, nu: (jnp.minimum(b + 1, nb - 1), 0, 0),
                       memory_space=pltpu.SMEM)
    any_spec = pl.BlockSpec(memory_space=pl.ANY)
    tok3 = slot_tok.reshape(nb, 1, tm)
    dst3 = slot_dst.reshape(nb, 1, tm)
    pad1 = (n_rows + tm + jnp.arange(tm, dtype=jnp.int32)).reshape(1, 1, tm)
    dstp3 = jnp.concatenate([pad1, dst3[:-1]], axis=0)
    return pl.pallas_call(
        functools.partial(_expert_kernel, tm=tm, unroll=MOE_DMA_UNROLL, pad_row0=n_rows,
                          n_chunk=MOE_N_CHUNK),
        grid_spec=pltpu.PrefetchScalarGridSpec(
            num_scalar_prefetch=2,
            grid=(nb,),
            in_specs=[cur, nxt, cur, cur, any_spec,
                      pl.BlockSpec((None, d, ed), lambda b, be, nu: (be[b], 0, 0)),
                      pl.BlockSpec((None, d, ed), lambda b, be, nu: (be[b], 0, 0)),
                      pl.BlockSpec((None, ed, d), lambda b, be, nu: (be[b], 0, 0))],
            out_specs=any_spec,
            scratch_shapes=[pltpu.VMEM((2, tm, d), F32), pltpu.VMEM((2, tm, d), F32),
                            pltpu.VMEM((d, ed), BF16), pltpu.VMEM((d, ed), BF16),
                            pltpu.VMEM((ed, d), BF16),
                            pltpu.SemaphoreType.DMA((2,)), pltpu.SemaphoreType.DMA((2,))]),
        out_shape=jax.ShapeDtypeStruct((n_rows + 2 * tm, d), F32),
        compiler_params=_cparams(1, 56),
        name="experts",
    )(block_e, n_used, tok3, tok3, dstp3, dst3, h2, w_gate, w_up, w_down)


def _dispatch_meta(eidx, n_experts, tm):
    n = eidx.shape[0]
    nk = n * TOP_K
    nb = -(-nk // tm) + n_experts
    flat_e = eidx.reshape(-1)
    order = jnp.argsort(flat_e, stable=True).astype(jnp.int32)
    counts = jnp.sum((flat_e[:, None] == jnp.arange(n_experts, dtype=jnp.int32)[None, :])
                     .astype(jnp.int32), axis=0)
    starts = jnp.cumsum(counts) - counts
    padded = (counts + tm - 1) // tm * tm
    pends = jnp.cumsum(padded)
    pstarts = pends - padded
    n_used = (pends[-1] // tm).astype(jnp.int32)
    blk = jnp.arange(nb, dtype=jnp.int32)
    blk_start = blk * tm
    be = jnp.minimum(jnp.sum((pends[None, :] <= blk_start[:, None]).astype(jnp.int32), axis=1),
                     n_experts - 1)
    be = jnp.where(blk < n_used, be, be[jnp.maximum(n_used - 1, 0)]).astype(jnp.int32)
    within = (blk_start - pstarts[be])[:, None] + jnp.arange(tm, dtype=jnp.int32)[None, :]
    valid = within < counts[be][:, None]
    flat = order[jnp.clip(starts[be][:, None] + within, 0, nk - 1)]
    tok = flat // TOP_K
    pad_row = nk + (blk & 1)[:, None] * tm + jnp.arange(tm, dtype=jnp.int32)[None, :]
    dst = jnp.where(valid, (flat % TOP_K) * n + tok, pad_row)
    return be, n_used.reshape(1), tok.astype(jnp.int32), dst.astype(jnp.int32)


def _combine_kernel(*refs, d, sd):
    y_refs = refs[:TOP_K]
    wt_ref, h_ref, x_ref, g2_ref, fg_ref, wgu_ref, wd_ref, o_ref = refs[TOP_K:]
    wt = wt_ref[...]
    routed = y_refs[0][...] * wt[:, 0:1]
    for k in range(1, TOP_K):
        routed = routed + y_refs[k][...] * wt[:, k:k + 1]
    hb = h_ref[...].astype(BF16)
    gu = jnp.dot(hb, wgu_ref[...], preferred_element_type=F32)
    act = (_silu(gu[:, :sd]) * gu[:, sd:]).astype(BF16)
    ffn = routed + jnp.dot(act, wd_ref[...], preferred_element_type=F32)
    x2 = x_ref[...] + g2_ref[...] * ffn
    o_ref[...] = _rms(x2, fg_ref[...])


def _combine(y, n_all, row0, wts, h2, x1, gate2, final_g, ws_gu, ws_d, tm):
    n, d = x1.shape
    sd = ws_d.shape[0]
    rb0 = row0 // tm
    kb = n_all // tm
    row = lambda w: pl.BlockSpec((tm, w), lambda i: (i, 0))
    shifted = lambda w: pl.BlockSpec((tm, w), lambda i: (rb0 + i, 0))
    y_specs = [pl.BlockSpec((tm, d), functools.partial(lambda i, k: (k * kb + rb0 + i, 0), k=k))
               for k in range(TOP_K)]
    return pl.pallas_call(
        functools.partial(_combine_kernel, d=d, sd=sd),
        grid=(n // tm,),
        in_specs=y_specs + [shifted(TOP_K), shifted(d), row(d), _mod_spec(gate2, tm, d),
                            pl.BlockSpec((1, d), lambda i: (0, 0)),
                            _resident(ws_gu.shape), _resident(ws_d.shape)],
        out_specs=row(d),
        out_shape=jax.ShapeDtypeStruct((n, d), F32),
        compiler_params=_cparams(1, 48),
        name="combine",
    )(*([y] * TOP_K), wts, h2, x1, gate2, final_g.reshape(1, d), ws_gu, ws_d)


def _in_proj(h, w_in, pw, aw, d, tm, q_dtype, want_bf16_kv):
    tn = 1024
    ident = lambda acc: (acc,)
    (u,) = _mm(h, w_in, 0, pw, [F32], ident, tm, tn, "proj_u")
    qscale = LOG2E / math.sqrt(HEAD_DIM)
    (q,) = _mm(h, w_in, pw, aw, [q_dtype], lambda acc: (acc * qscale,), tm, tn, "proj_q")
    kv_dt = [F32, BF16] if want_bf16_kv else [F32]
    both = (lambda acc: (acc, acc)) if want_bf16_kv else ident
    k = _mm(h, w_in, pw + aw, aw, kv_dt, both, tm, tn, "proj_k")
    v = _mm(h, w_in, pw + 2 * aw, aw, kv_dt, both, tm, tn, "proj_v")
    (gates,) = _mm(h, w_in, pw + 3 * aw, 2 * d, [BF16], lambda acc: (jax.nn.sigmoid(acc),),
                   tm, tn, "proj_gates")
    return u, q, k, v, gates


def kernel(x_prompt, x_sample, c_prompt, c_sample, state_pool, cache_k, cache_v, page_table,
           norm1_g, norm2_g, w_ada, b_ada, w_in, pool_mix, pool_scale, w_br_pool, w_br_attn,
           sb_bias, w_o, w_router, router_bias, w_e_gate, w_e_up, w_e_down, w_s_gate, w_s_up,
           w_s_down, final_g):
    depth = w_in.shape[0]
    nbp, seq, d = x_prompt.shape
    nbs, t, _ = x_sample.shape
    assert nbp == 1 and depth == 1 and t & (t - 1) == 0
    l = 0
    pw = pool_scale.shape[1]
    aw = w_br_attn.shape[1]
    nh = aw // HEAD_DIM
    n_experts = w_router.shape[2]
    n_pages, page = page_table.shape[1], cache_k.shape[2]
    past_len = n_pages * page
    ns = nbs * t
    n_all = seq + ns
    buf = POOL_HIST - 1

    xp = x_prompt.reshape(seq, d)
    xs = x_sample.reshape(ns, d)
    rows = nbp + nbs
    rpad = -(-rows // 8) * 8
    c_all = jnp.concatenate([c_prompt, c_sample, jnp.zeros((rpad - rows, d), F32)], axis=0)

    mod = _ada(c_all, w_ada[l], b_ada[l])
    mod_p = [mod[0:1, i * d:(i + 1) * d] for i in range(N_MOD)]
    mod_s = [jnp.repeat(mod[nbp:rows, i * d:(i + 1) * d], t, axis=0) for i in range(N_MOD)]
    wbp, wba, wo = (w.astype(BF16) for w in (w_br_pool[l], w_br_attn[l], w_o[l]))
    wr_t = w_router[l].T.astype(BF16)
    ws_gu = jnp.concatenate([w_s_gate[l], w_s_up[l]], axis=1).astype(BF16)
    ws_d = w_s_down[l].astype(BF16)
    bias2 = sb_bias[l] * LOG2E

    hp = _normmod(xp, norm1_g[l], mod_p[1], mod_p[0], 512)
    u_p, q_p, (k_p, kb_p), (v_p, vb_p), gates_p = _in_proj(hp, w_in[l], pw, aw, d, 1024, BF16, True)
    pool_p = _pool_prompt(u_p, pool_mix[l], pool_scale[l])
    attn_p = _attn_prompt(q_p, kb_p, vb_p, bias2)
    x1_p, h2 = _merge(pool_p, attn_p, gates_p, xp, mod_p[2], norm2_g[l], mod_p[4], mod_p[3],
                      wbp, wba, wo, 256, n_all, 0)

    hs = _normmod(xs, norm1_g[l], mod_s[1], mod_s[0], ns)
    u_s, q_s, (k_s,), (v_s,), gates_s = _in_proj(hs, w_in[l], pw, aw, d, ns, F32, False)
    hist_s = jnp.concatenate([jnp.zeros((nbs, 1, pw), F32), state_pool[l]], axis=1)
    pool_s = _pool_sample(u_s.reshape(nbs, t, pw), hist_s, pool_mix[l], pool_scale[l], past_len)
    attn_s = _attn_sample(q_s, k_s, v_s, cache_k[l], cache_v[l], page_table, bias2, t)
    x1_s, h2 = _merge(pool_s, attn_s, gates_s, xs, mod_s[2], norm2_g[l], mod_s[4], mod_s[3],
                      wbp, wba, wo, ns, n_all, seq, h2_prev=h2)

    ei, wt = _router(h2, wr_t, router_bias[l], 768)
    be, n_used, slot_tok, slot_dst = _dispatch_meta(ei.T, n_experts, MOE_TM)
    y = _experts(be, n_used, slot_tok, slot_dst, h2, w_e_gate[l], w_e_up[l], w_e_down[l])
    wts = wt.T
    yp = _combine(y, n_all, 0, wts, h2, x1_p, mod_p[5], final_g, ws_gu, ws_d, 128)
    ys = _combine(y, n_all, seq, wts, h2, x1_s, mod_s[5], final_g, ws_gu, ws_d, 128)

    heads = lambda a, b_: a.reshape(1, b_, -1, nh, HEAD_DIM)
    pool_state_s = jnp.concatenate([state_pool[l][:, t:], u_s.reshape(nbs, t, pw)], axis=1)
    return (yp.reshape(nbp, seq, d), ys.reshape(nbs, t, d),
            u_p[seq - buf:].reshape(1, nbp, buf, pw), heads(k_p, nbp), heads(v_p, nbp),
            pool_state_s[None], heads(k_s, nbs), heads(v_s, nbs))
```

```python
import functools
import math

import jax
import jax.numpy as jnp
from jax import lax
from jax.experimental import pallas as pl
from jax.experimental.pallas import tpu as pltpu

F32 = jnp.float32
BF16 = jnp.bfloat16

HEAD_DIM = 128
POOL_WINDOWS = (2, 4, 8, 16)
POOL_HIST = 16
N_EXPERT_GROUPS = 8
TOPK_GROUPS = 4
TOP_K = 8
ROUTED_SCALE = 2.5
RMS_EPS = 1e-6
N_MOD = 6
LOG2E = 1.4426950408889634

MOE_TM = 256
MOE_DMA_UNROLL = 8
MOE_N_CHUNK = 2
ATTN_TQ = 512
ATTN_TK = 256
PAGES_PER_STEP = 16
PAGE_SKEW = 2
PAGE_HEAD_GROUP = 2
MIB = 1024 * 1024


def _cparams(n_axes, vmem_mib):
    return pltpu.CompilerParams(dimension_semantics=("arbitrary",) * n_axes,
                                vmem_limit_bytes=vmem_mib * MIB)


def _silu(x):
    return x * jax.nn.sigmoid(x)


def _resident(shape):
    return pl.BlockSpec(shape, lambda *_: (0,) * len(shape), pipeline_mode=pl.Buffered(1))


def _ada_kernel(c_ref, w_ref, b_ref, o_ref):
    s = _silu(c_ref[...]).astype(BF16)
    o_ref[...] = jnp.dot(s, w_ref[...].astype(BF16), preferred_element_type=F32) + b_ref[...]


def _ada(c_all, w_ada, b_ada, tn=512):
    r, d = c_all.shape
    n = w_ada.shape[1]
    return pl.pallas_call(
        _ada_kernel,
        grid=(n // tn,),
        in_specs=[pl.BlockSpec((r, d), lambda j: (0, 0)),
                  pl.BlockSpec((d, tn), lambda j: (0, j)),
                  pl.BlockSpec((1, tn), lambda j: (0, j))],
        out_specs=pl.BlockSpec((r, tn), lambda j: (0, j)),
        out_shape=jax.ShapeDtypeStruct((r, n), F32),
        compiler_params=_cparams(1, 32),
        name="ada_mod",
    )(c_all, w_ada, b_ada.reshape(1, n))


def _rms(x, g):
    return x * lax.rsqrt(jnp.mean(x * x, axis=-1, keepdims=True) + RMS_EPS) * g


def _normmod_kernel(x_ref, g_ref, sc_ref, sh_ref, o_ref):
    xn = _rms(x_ref[...], g_ref[...])
    o_ref[...] = (xn * (1.0 + sc_ref[...]) + sh_ref[...]).astype(o_ref.dtype)


def _mod_spec(mod, tm, d):
    if mod.shape[0] == 1:
        return pl.BlockSpec((1, d), lambda i: (0, 0))
    return pl.BlockSpec((tm, d), lambda i: (i, 0))


def _normmod(x, g, scale, shift, tm):
    n, d = x.shape
    return pl.pallas_call(
        _normmod_kernel,
        grid=(n // tm,),
        in_specs=[pl.BlockSpec((tm, d), lambda i: (i, 0)),
                  pl.BlockSpec((1, d), lambda i: (0, 0)),
                  _mod_spec(scale, tm, d), _mod_spec(shift, tm, d)],
        out_specs=pl.BlockSpec((tm, d), lambda i: (i, 0)),
        out_shape=jax.ShapeDtypeStruct((n, d), BF16),
        compiler_params=_cparams(1, 32),
        name="norm_mod",
    )(x, g.reshape(1, d), scale, shift)


def _mm_kernel(a_ref, w_ref, *refs, epilogue, n_out):
    out_refs, wb_ref = refs[:n_out], refs[n_out]

    @pl.when(pl.program_id(1) == 0)
    def _():
        wb_ref[...] = w_ref[...].astype(BF16)

    acc = jnp.dot(a_ref[...], wb_ref[...], preferred_element_type=F32)
    for r, v in zip(out_refs, epilogue(acc)):
        r[...] = v.astype(r.dtype)


def _mm(a, w, col0, ncols, out_dtypes, epilogue, tm, tn, name):
    m, k = a.shape
    cb0 = col0 // tn
    out_spec = pl.BlockSpec((tm, tn), lambda j, i: (i, j))
    return pl.pallas_call(
        functools.partial(_mm_kernel, epilogue=epilogue, n_out=len(out_dtypes)),
        grid=(ncols // tn, m // tm),
        in_specs=[pl.BlockSpec((tm, k), lambda j, i: (i, 0)),
                  pl.BlockSpec((k, tn), lambda j, i: (0, cb0 + j))],
        out_specs=[out_spec] * len(out_dtypes),
        out_shape=[jax.ShapeDtypeStruct((m, ncols), dt) for dt in out_dtypes],
        scratch_shapes=[pltpu.VMEM((k, tn), BF16)],
        compiler_params=_cparams(2, 56),
        name=name,
    )(a, w)


def _pool_groups(win_fn, u_fn, pos, mix_ref, scale_ref, store_fn, gw):
    for g, w in enumerate(POOL_WINDOWS):
        lo = g * gw
        win = win_fn(0, lo)
        for dlt in range(1, w):
            win = win + win_fn(dlt, lo)
        cnt = jnp.minimum(pos + 1, w).astype(F32)
        dd = win / cnt - u_fn(lo)
        mixed = jnp.dot(dd.astype(BF16), mix_ref[g].astype(BF16), preferred_element_type=F32)
        store_fn(lo, mixed * scale_ref[:, lo:lo + gw])


def _pool_prompt_kernel(hist_ref, u_ref, mix_ref, scale_ref, o_ref, ext_ref, *, tm, gw):
    i = pl.program_id(0)
    ext_ref[0:POOL_HIST, :] = jnp.where(i == 0, 0.0, hist_ref[...])
    ext_ref[POOL_HIST:POOL_HIST + tm, :] = u_ref[...]
    pos = i * tm + lax.broadcasted_iota(jnp.int32, (tm, 1), 0)

    def store(lo, val):
        o_ref[:, lo:lo + gw] = val.astype(o_ref.dtype)

    _pool_groups(lambda dlt, lo: ext_ref[pl.ds(POOL_HIST - dlt, tm), lo:lo + gw],
                 lambda lo: u_ref[:, lo:lo + gw], pos, mix_ref, scale_ref, store, gw)


def _pool_prompt(u, pool_mix, pool_scale, tm=512):
    n, pw = u.shape
    ng, gw, _ = pool_mix.shape
    hb = tm // POOL_HIST
    return pl.pallas_call(
        functools.partial(_pool_prompt_kernel, tm=tm, gw=gw),
        grid=(n // tm,),
        in_specs=[pl.BlockSpec((POOL_HIST, pw), lambda i: (jnp.maximum(i * hb - 1, 0), 0)),
                  pl.BlockSpec((tm, pw), lambda i: (i, 0)),
                  pl.BlockSpec((ng, gw, gw), lambda i: (0, 0, 0)),
                  pl.BlockSpec((1, pw), lambda i: (0, 0))],
        out_specs=pl.BlockSpec((tm, pw), lambda i: (i, 0)),
        out_shape=jax.ShapeDtypeStruct((n, pw), BF16),
        scratch_shapes=[pltpu.VMEM((POOL_HIST + tm, pw), F32)],
        compiler_params=_cparams(1, 32),
        name="pool_prompt",
    )(u, u, pool_mix, pool_scale.reshape(1, pw))


def _pool_sample_kernel(hist_ref, u_ref, mix_ref, scale_ref, o_ref, ext_ref, *, nb, t, gw, past_len):
    ext_ref[:, 0:POOL_HIST, :] = hist_ref[...]
    ext_ref[:, POOL_HIST:POOL_HIST + t, :] = u_ref[...]
    pos = past_len + (lax.broadcasted_iota(jnp.int32, (nb * t, 1), 0) & (t - 1))

    def store(lo, val):
        o_ref[:, lo:lo + gw] = val.astype(o_ref.dtype)

    _pool_groups(lambda dlt, lo: ext_ref[:, pl.ds(POOL_HIST - dlt, t), lo:lo + gw].reshape(nb * t, gw),
                 lambda lo: u_ref[:, :, lo:lo + gw].reshape(nb * t, gw), pos, mix_ref, scale_ref,
                 store, gw)


def _pool_sample(u3, hist3, pool_mix, pool_scale, past_len):
    nb, t, pw = u3.shape
    ng, gw, _ = pool_mix.shape
    return pl.pallas_call(
        functools.partial(_pool_sample_kernel, nb=nb, t=t, gw=gw, past_len=past_len),
        grid=(1,),
        in_specs=[pl.BlockSpec((nb, POOL_HIST, pw), lambda i: (0, 0, 0)),
                  pl.BlockSpec((nb, t, pw), lambda i: (0, 0, 0)),
                  pl.BlockSpec((ng, gw, gw), lambda i: (0, 0, 0)),
                  pl.BlockSpec((1, pw), lambda i: (0, 0))],
        out_specs=pl.BlockSpec((nb * t, pw), lambda i: (0, 0)),
        out_shape=jax.ShapeDtypeStruct((nb * t, pw), BF16),
        scratch_shapes=[pltpu.VMEM((nb, POOL_HIST + t, pw), F32)],
        compiler_params=_cparams(1, 32),
        name="pool_sample",
    )(hist3, u3, pool_mix, pool_scale.reshape(1, pw))


def _softplus2(y):
    return jnp.maximum(y, 0.0) + jnp.log2(1.0 + jnp.exp2(-jnp.abs(y)))


def _suffix_matrix(nk, nrep):
    r = lax.broadcasted_iota(jnp.int32, (nk, nk + nrep), 0)
    c = lax.broadcasted_iota(jnp.int32, (nk, nk + nrep), 1)
    return jnp.where((r > c) | (c >= nk), 1.0, 0.0).astype(BF16)


def _attn_prompt_kernel(bias_ref, q_ref, k_ref, v_ref, o_ref, acc_ref, car_ref, ls_ref, r_ref,
                        *, tq, tk):
    h = pl.program_id(0)
    i = pl.program_id(1)
    bias = bias_ref[h]
    q = q_ref[...]
    nsub = tq // tk
    sufm = _suffix_matrix(tk, 0)
    row = lax.broadcasted_iota(jnp.int32, (tq, tk), 0)
    col = lax.broadcasted_iota(jnp.int32, (tq, tk), 1)

    def sub_blocks(j):
        for u in range(nsub):
            off = (nsub - 1 - u) * tk
            yield u, off, pl.multiple_of(j * tq + off, tk)

    def scores(j, masked, slot):
        for u, off, start in sub_blocks(j):
            y = lax.dot_general(q, k_ref[pl.ds(start, tk), :], (((1,), (1,)), ((), ())),
                                preferred_element_type=F32) + bias
            s2 = _softplus2(y)
            lk = jnp.where(col + off < row, s2, 0.0) if masked else s2
            r_ref[slot, u, :, 0:tk] = jnp.dot(lk.astype(BF16), sufm, preferred_element_type=F32)
            r_ref[slot, u, :, tk:tk + HEAD_DIM] = jnp.broadcast_to(
                jnp.sum(lk, axis=1, keepdims=True), (tq, HEAD_DIM))
            ls_ref[slot, u] = y - s2

    def weights(j, masked, slot):
        car = car_ref[...]
        acc = acc_ref[...]
        for u, off, start in sub_blocks(j):
            a = jnp.exp2(ls_ref[slot, u] - r_ref[slot, u, :, 0:tk]
                         - jnp.concatenate([car] * (tk // HEAD_DIM), axis=1))
            if masked:
                a = jnp.where(col + off < row, a, 0.0)
            acc = acc + jnp.dot(a.astype(BF16), v_ref[pl.ds(start, tk), :],
                                preferred_element_type=F32)
            car = car + r_ref[slot, u, :, tk:tk + HEAD_DIM]
        acc_ref[...] = acc
        car_ref[...] = car

    acc_ref[...] = jnp.zeros_like(acc_ref)
    car_ref[...] = jnp.zeros_like(car_ref)
    scores(i, True, 0)

    @pl.when(i == 0)
    def _():
        weights(0, True, 0)

    @pl.when(i > 0)
    def _():
        weights(i, True, 0)
        scores(i - 1, False, 1)

        def body(s, c):
            j = i - 1 - s
            slot = (s + 1) & 1
            weights(j, False, slot)
            scores(j - 1, False, 1 - slot)
            return c

        lax.fori_loop(0, i - 1, body, 0)
        weights(0, False, i & 1)

    o_ref[...] = acc_ref[...].astype(o_ref.dtype)


def _attn_prompt(qs, kb, vb, bias2, tq=ATTN_TQ, tk=ATTN_TK):
    s, aw = qs.shape
    nh = aw // HEAD_DIM
    nsub = tq // tk
    return pl.pallas_call(
        functools.partial(_attn_prompt_kernel, tq=tq, tk=tk),
        grid_spec=pltpu.PrefetchScalarGridSpec(
            num_scalar_prefetch=1,
            grid=(nh, s // tq),
            in_specs=[pl.BlockSpec((tq, HEAD_DIM), lambda h, i, b: (i, h)),
                      pl.BlockSpec((s, HEAD_DIM), lambda h, i, b: (0, h)),
                      pl.BlockSpec((s, HEAD_DIM), lambda h, i, b: (0, h))],
            out_specs=pl.BlockSpec((tq, HEAD_DIM), lambda h, i, b: (i, h)),
            scratch_shapes=[pltpu.VMEM((tq, HEAD_DIM), F32), pltpu.VMEM((tq, HEAD_DIM), F32),
                            pltpu.VMEM((2, nsub, tq, tk), F32),
                            pltpu.VMEM((2, nsub, tq, tk + HEAD_DIM), F32)]),
        out_shape=jax.ShapeDtypeStruct((s, aw), BF16),
        compiler_params=_cparams(2, 32),
        name="attn_prompt",
    )(bias2, qs, kb, vb)


def _attn_sample_kernel(pt_ref, bias_ref, q_ref, kn_ref, vn_ref, ck_ref, cv_ref, o_ref,
                        acc_ref, car_ref, kpad_ref, vpad_ref, kbuf, vbuf, sem,
                        *, pps, nh, t, skew, n_pages, hg):
    b = pl.program_id(0)
    c = pl.program_id(1)
    nc = pl.num_programs(1)
    step = b * nc + c
    slot = step & 1
    ht = nh * t
    page = HEAD_DIM
    sufm = _suffix_matrix(page, HEAD_DIM)
    q = q_ref[...]
    head = lambda h: slice(h * HEAD_DIM, (h + 1) * HEAD_DIM)
    rows = lambda h: slice(h * t, (h + 1) * t)
    qall = jnp.concatenate([q[:, head(h)] for h in range(nh)], axis=0).astype(BF16)
    bias_rows = jnp.concatenate([jnp.full((t, page), bias_ref[h], F32) for h in range(nh)], axis=0)
    kpos = lax.broadcasted_iota(jnp.int32, (ht, page), 1)
    qpos = lax.broadcasted_iota(jnp.int32, (ht, page), 0) & (t - 1)

    def page_copies(bb, cc, sl):
        cps = []
        for i in range(pps):
            pg = pt_ref[bb, n_pages - 1 - (cc * pps + i)]
            for h in range(nh):
                cps.append(pltpu.make_async_copy(ck_ref.at[pg, :, h, :], kbuf.at[sl, i, h], sem.at[sl]))
                cps.append(pltpu.make_async_copy(cv_ref.at[pg, :, h, :], vbuf.at[sl, i, h], sem.at[sl]))
        return cps

    def scores(k3, masked):
        k2 = k3.reshape(nh * page, HEAD_DIM).astype(BF16)
        ys = []
        for g in range(nh // hg):
            z = lax.dot_general(qall[g * hg * t:(g + 1) * hg * t], k2[g * hg * page:(g + 1) * hg * page],
                                (((1,), (1,)), ((), ())), preferred_element_type=F32)
            ys += [z[rows(h), h * page:(h + 1) * page] for h in range(hg)]
        y = jnp.concatenate(ys, axis=0) + bias_rows
        s2 = _softplus2(y)
        lk = jnp.where(kpos < qpos, s2, 0.0) if masked else s2
        return y - s2, jnp.dot(lk.astype(BF16), sufm, preferred_element_type=F32)

    def weights(v3, ls, r, car, acc, masked):
        a = jnp.exp2(ls - r[:, :page] - car)
        if masked:
            a = jnp.where(kpos < qpos, a, 0.0)
        zero = jnp.zeros((t, page), F32)
        v2 = v3.reshape(nh * page, HEAD_DIM).astype(BF16)
        outs = []
        for g in range(nh // hg):
            a_bd = jnp.concatenate(
                [jnp.concatenate([a[rows(g * hg + h)] if gg == h else zero for gg in range(hg)], axis=1)
                 for h in range(hg)], axis=0).astype(BF16)
            outs.append(jnp.dot(a_bd, v2[g * hg * page:(g + 1) * hg * page], preferred_element_type=F32))
        return car + r[:, page:], acc + jnp.concatenate(outs, axis=0)

    def process(pages, masked):
        car = car_ref[...]
        acc = acc_ref[...]
        staged = []
        for idx in range(len(pages) + skew):
            if idx < len(pages):
                staged.append(scores(pages[idx][0](), masked))
            if idx >= skew:
                car, acc = weights(pages[idx - skew][1](), *staged[idx - skew], car, acc, masked)
        car_ref[...] = car
        acc_ref[...] = acc

    def start_all(cps):
        for n, cp in enumerate(cps):
            cp.start(priority=n & 1)

    @pl.when(step == 0)
    def _():
        start_all(page_copies(0, 0, 0))

    @pl.when(step + 1 < pl.num_programs(0) * nc)
    def _():
        wrap = c + 1 == nc
        start_all(page_copies(jnp.where(wrap, b + 1, b), jnp.where(wrap, 0, c + 1), 1 - slot))

    for cp in page_copies(b, c, slot):
        cp.wait()

    @pl.when(c == 0)
    def _():
        acc_ref[...] = jnp.zeros_like(acc_ref)
        car_ref[...] = jnp.zeros_like(car_ref)
        kpad_ref[...] = jnp.zeros_like(kpad_ref)
        vpad_ref[...] = jnp.zeros_like(vpad_ref)
        for h in range(nh):
            kpad_ref[h, 0:t, :] = kn_ref[:, head(h)]
            vpad_ref[h, 0:t, :] = vn_ref[:, head(h)]
        process([(lambda: kpad_ref[...], lambda: vpad_ref[...])], True)

    process([(functools.partial(lambda i: kbuf[slot, i], i), functools.partial(lambda i: vbuf[slot, i], i))
             for i in range(pps)], False)

    @pl.when(c == nc - 1)
    def _():
        acc = acc_ref[...]
        for h in range(nh):
            o_ref[:, head(h)] = acc[rows(h)]


def _attn_sample(q, kn, vn, cache_k, cache_v, page_table, bias2, t, pps=PAGES_PER_STEP):
    nbt, aw = q.shape
    nb = nbt // t
    nh = aw // HEAD_DIM
    n_pages = page_table.shape[1]
    page = cache_k.shape[1]
    assert cache_k.shape[1:] == (HEAD_DIM, nh, HEAD_DIM) and n_pages % pps == 0
    row_spec = pl.BlockSpec((t, aw), lambda b, c, pt, bias: (b, 0))
    any_spec = pl.BlockSpec(memory_space=pl.ANY)
    return pl.pallas_call(
        functools.partial(_attn_sample_kernel, pps=pps, nh=nh, t=t, skew=PAGE_SKEW, n_pages=n_pages,
                          hg=PAGE_HEAD_GROUP),
        grid_spec=pltpu.PrefetchScalarGridSpec(
            num_scalar_prefetch=2,
            grid=(nb, n_pages // pps),
            in_specs=[row_spec, row_spec, row_spec, any_spec, any_spec],
            out_specs=row_spec,
            scratch_shapes=[pltpu.VMEM((nh * t, HEAD_DIM), F32), pltpu.VMEM((nh * t, HEAD_DIM), F32),
                            pltpu.VMEM((nh, page, HEAD_DIM), F32),
                            pltpu.VMEM((nh, page, HEAD_DIM), F32),
                            pltpu.VMEM((2, pps, nh, page, HEAD_DIM), F32),
                            pltpu.VMEM((2, pps, nh, page, HEAD_DIM), F32),
                            pltpu.SemaphoreType.DMA((2,))]),
        out_shape=jax.ShapeDtypeStruct((nbt, aw), F32),
        compiler_params=_cparams(2, 56),
        name="attn_sample",
    )(page_table, bias2, q, kn, vn, cache_k, cache_v)


def _merge_kernel(po_ref, at_ref, gt_ref, x_ref, g1_ref, n2_ref, sc_ref, sh_ref,
                  wbp_ref, wba_ref, wo_ref, *refs, d, n_blocks):
    x1_ref, h2_ref = refs[-2:]
    i = pl.program_id(0)

    @pl.when(i < n_blocks)
    def _():
        bp = jnp.dot(po_ref[...], wbp_ref[...], preferred_element_type=F32)
        ba = jnp.dot(at_ref[...].astype(BF16), wba_ref[...], preferred_element_type=F32)
        merged = gt_ref[:, :d].astype(F32) * bp + gt_ref[:, d:].astype(F32) * ba
        x1 = x_ref[...] + g1_ref[...] * jnp.dot(merged.astype(BF16), wo_ref[...],
                                                preferred_element_type=F32)
        x1_ref[...] = x1
        h2_ref[...] = _rms(x1, n2_ref[...]) * (1.0 + sc_ref[...]) + sh_ref[...]

    @pl.when(i >= n_blocks)
    def _():
        h2_ref[...] = jnp.zeros_like(h2_ref)


def _merge(pool_out, attn, gates, x, gate1, norm2_g, scale2, shift2, wbp, wba, wo, tm,
           n_all, row0, h2_prev=None):
    n, d = x.shape
    pw, aw = pool_out.shape[1], attn.shape[1]
    rb0 = row0 // tm
    n_blocks = n // tm
    steps = n_blocks if h2_prev is not None else (n_all - row0) // tm
    blk = lambda i: jnp.minimum(i, n_blocks - 1)
    row = lambda w: pl.BlockSpec((tm, w), lambda i: (blk(i), 0))
    mod = lambda m: (pl.BlockSpec((1, d), lambda i: (0, 0)) if m.shape[0] == 1 else row(d))
    in_specs = [row(pw), row(aw), row(2 * d), row(d), mod(gate1),
                pl.BlockSpec((1, d), lambda i: (0, 0)), mod(scale2), mod(shift2),
                _resident(wbp.shape), _resident(wba.shape), _resident(wo.shape)]
    args = [pool_out, attn, gates, x, gate1, norm2_g.reshape(1, d), scale2, shift2, wbp, wba, wo]
    aliases = {}
    if h2_prev is not None:
        in_specs.append(pl.BlockSpec(memory_space=pl.ANY))
        args.append(h2_prev)
        aliases = {len(args) - 1: 1}
    return pl.pallas_call(
        functools.partial(_merge_kernel, d=d, n_blocks=n_blocks),
        grid=(steps,),
        in_specs=in_specs,
        out_specs=[row(d), pl.BlockSpec((tm, d), lambda i: (rb0 + i, 0))],
        out_shape=[jax.ShapeDtypeStruct((n, d), F32), jax.ShapeDtypeStruct((n_all, d), F32)],
        input_output_aliases=aliases,
        compiler_params=_cparams(1, 56),
        name="merge",
    )(*args)


def _first_max(v, ids, n):
    m = jnp.max(v, axis=0, keepdims=True)
    idx = jnp.min(jnp.where(v == m, ids, n), axis=0, keepdims=True)
    return m, idx


def _router_kernel(h_ref, wr_ref, rb_ref, ei_ref, wt_ref, *, ne, ng, tkg, topk):
    epg = ne // ng
    hb = h_ref[...].astype(BF16)
    logits = lax.dot_general(wr_ref[...], hb, (((1,), (1,)), ((), ())), preferred_element_type=F32)
    scores = jax.nn.sigmoid(logits)
    sel = scores + rb_ref[...]
    tm = sel.shape[1]
    neg = -jnp.inf
    fiota = lambda n: lax.broadcasted_iota(jnp.int32, (n, tm), 0).astype(F32)
    ids_g = fiota(epg)
    gs = []
    for g in range(ng):
        v = sel[g * epg:(g + 1) * epg, :]
        m1, i1 = _first_max(v, ids_g, float(epg))
        m2 = jnp.max(jnp.where(ids_g == i1, neg, v), axis=0, keepdims=True)
        gs.append(m1 + m2)
    gsc = jnp.concatenate(gs, axis=0)
    ids_n = fiota(ng)
    gmask = jnp.zeros((ng, tm), F32)
    for _ in range(tkg):
        _, gi = _first_max(gsc, ids_n, float(ng))
        hit = ids_n == gi
        gmask = jnp.where(hit, 1.0, gmask)
        gsc = jnp.where(hit, neg, gsc)
    emask = jnp.concatenate(
        [jnp.broadcast_to(gmask[g:g + 1, :], (epg, tm)) for g in range(ng)], axis=0)
    cand = jnp.where(emask > 0.5, sel, neg)
    ids_e = fiota(ne)
    eis, wts = [], []
    for _ in range(topk):
        _, ei = _first_max(cand, ids_e, float(ne))
        hit = ids_e == ei
        eis.append(ei)
        wts.append(jnp.sum(jnp.where(hit, scores, 0.0), axis=0, keepdims=True))
        cand = jnp.where(hit, neg, cand)
    w = jnp.concatenate(wts, axis=0)
    ei_ref[...] = jnp.concatenate(eis, axis=0).astype(jnp.int32)
    wt_ref[...] = w / jnp.sum(w, axis=0, keepdims=True) * ROUTED_SCALE


def _router(h2, wr_t, router_bias, tm):
    n, d = h2.shape
    ne = wr_t.shape[0]
    return pl.pallas_call(
        functools.partial(_router_kernel, ne=ne, ng=N_EXPERT_GROUPS, tkg=TOPK_GROUPS, topk=TOP_K),
        grid=(n // tm,),
        in_specs=[pl.BlockSpec((tm, d), lambda i: (i, 0)),
                  pl.BlockSpec((ne, d), lambda i: (0, 0)),
                  pl.BlockSpec((ne, 1), lambda i: (0, 0))],
        out_specs=[pl.BlockSpec((TOP_K, tm), lambda i: (0, i)),
                   pl.BlockSpec((TOP_K, tm), lambda i: (0, i))],
        out_shape=[jax.ShapeDtypeStruct((TOP_K, n), jnp.int32),
                   jax.ShapeDtypeStruct((TOP_K, n), F32)],
        compiler_params=_cparams(1, 32),
        name="router",
    )(h2, wr_t, router_bias.reshape(ne, 1))


def _expert_kernel(be_ref, nu_ref, tok_ref, tokn_ref, dstp_ref, dst_ref, h_ref, wg_ref, wu_ref,
                   wd_ref, y_ref, xbuf, ybuf, wgb, wub, wdb, gsem, ssem,
                   *, tm, unroll, pad_row0, n_chunk):
    b = pl.program_id(0)
    nu = nu_ref[0]
    slot = b & 1
    other = 1 - slot
    ed = wg_ref.shape[1]
    GATHER_PRIORITY, SCATTER_PRIORITY = 0, 1

    def gather(tref, s, r):
        return pltpu.make_async_copy(h_ref.at[pl.ds(tref[0, 0, r], 1)], xbuf.at[s, pl.ds(r, 1)],
                                     gsem.at[s])

    def scatter(row, s, r):
        return pltpu.make_async_copy(ybuf.at[s, pl.ds(r, 1)], y_ref.at[pl.ds(row, 1)], ssem.at[s])

    def wait_gather(s):
        pltpu.make_async_copy(h_ref.at[pl.ds(0, tm)], xbuf.at[s], gsem.at[s]).wait()

    def wait_scatter(s):
        pltpu.make_async_copy(ybuf.at[s], y_ref.at[pl.ds(0, tm)], ssem.at[s]).wait()

    def rolled(start_row):
        def body(i, c):
            for u in range(unroll):
                start_row(i * unroll + u)
            return c

        lax.fori_loop(0, tm // unroll, body, 0)

    @pl.when(b < nu)
    def _():
        @pl.when(b == 0)
        def _():
            ybuf[...] = jnp.zeros_like(ybuf)
            rolled(lambda r: gather(tok_ref, 0, r).start(GATHER_PRIORITY))
            rolled(lambda r: scatter(pad_row0 + r, 0, r).start(SCATTER_PRIORITY))

        @pl.when((b == 0) | (be_ref[b] != be_ref[jnp.maximum(b - 1, 0)]))
        def _():
            wgb[...] = wg_ref[...].astype(BF16)
            wub[...] = wu_ref[...].astype(BF16)
            wdb[...] = wd_ref[...].astype(BF16)

        wait_gather(slot)
        starts = ([functools.partial(lambda r: gather(tokn_ref, other, r).start(r & 1), r)
                   for r in range(tm)]
                  + [functools.partial(
                      lambda r: scatter(dstp_ref[0, 0, r], other, r).start(1 - (r & 1)), r)
                     for r in range(tm)])
        n_groups = 3 * n_chunk
        per = -(-len(starts) // n_groups)

        def issue_group(g):
            for st in starts[g * per:(g + 1) * per]:
                st()

        x = xbuf[slot].astype(BF16)
        cw = ed // n_chunk
        acts = []
        for c in range(n_chunk):
            cs = slice(c * cw, (c + 1) * cw)
            gate = jnp.dot(x, wgb[:, cs], preferred_element_type=F32)
            issue_group(2 * c)
            up = jnp.dot(x, wub[:, cs], preferred_element_type=F32)
            issue_group(2 * c + 1)
            acts.append((_silu(gate) * up).astype(BF16))
        y = None
        for c in range(n_chunk):
            part = jnp.dot(acts[c], wdb[c * cw:(c + 1) * cw, :], preferred_element_type=F32)
            y = part if y is None else y + part
            issue_group(2 * n_chunk + c)
        wait_scatter(slot)
        ybuf[slot] = y

        @pl.when(b == nu - 1)
        def _():
            wait_gather(other)
            rolled(lambda r: scatter(dst_ref[0, 0, r], slot, r).start(SCATTER_PRIORITY))
            wait_scatter(other)
            wait_scatter(slot)


def _experts(block_e, n_used, slot_tok, slot_dst, h2, w_gate, w_up, w_down, tm=MOE_TM):
    nb = block_e.shape[0]
    n_all, d = h2.shape
    ed = w_gate.shape[2]
    n_rows = n_all * TOP_K
    cur = pl.BlockSpec((1, 1, tm), lambda b, be, nu: (b, 0, 0), memory_space=pltpu.SMEM)
    nxt = pl.BlockSpec((1, 1, tm), lambda b, be, nu: (jnp.minimum(b + 1, nb - 1), 0, 0),
                       memory_space=pltpu.SMEM)
    any_spec = pl.BlockSpec(memory_space=pl.ANY)
    tok3 = slot_tok.reshape(nb, 1, tm)
    dst3 = slot_dst.reshape(nb, 1, tm)
    pad1 = (n_rows + tm + jnp.arange(tm, dtype=jnp.int32)).reshape(1, 1, tm)
    dstp3 = jnp.concatenate([pad1, dst3[:-1]], axis=0)
    return pl.pallas_call(
        functools.partial(_expert_kernel, tm=tm, unroll=MOE_DMA_UNROLL, pad_row0=n_rows,
                          n_chunk=MOE_N_CHUNK),
        grid_spec=pltpu.PrefetchScalarGridSpec(
            num_scalar_prefetch=2,
            grid=(nb,),
            in_specs=[cur, nxt, cur, cur, any_spec,
                      pl.BlockSpec((None, d, ed), lambda b, be, nu: (be[b], 0, 0)),
                      pl.BlockSpec((None, d, ed), lambda b, be, nu: (be[b], 0, 0)),
                      pl.BlockSpec((None, ed, d), lambda b, be, nu: (be[b], 0, 0))],
            out_specs=any_spec,
            scratch_shapes=[pltpu.VMEM((2, tm, d), F32), pltpu.VMEM((2, tm, d), F32),
                            pltpu.VMEM((d, ed), BF16), pltpu.VMEM((d, ed), BF16),
                            pltpu.VMEM((ed, d), BF16),
                            pltpu.SemaphoreType.DMA((2,)), pltpu.SemaphoreType.DMA((2,))]),
        out_shape=jax.ShapeDtypeStruct((n_rows + 2 * tm, d), F32),
        compiler_params=_cparams(1, 56),
        name="experts",
    )(block_e, n_used, tok3, tok3, dstp3, dst3, h2, w_gate, w_up, w_down)


def _dispatch_meta(eidx, n_experts, tm):
    n = eidx.shape[0]
    nk = n * TOP_K
    nb = -(-nk // tm) + n_experts
    flat_e = eidx.reshape(-1)
    order = jnp.argsort(flat_e, stable=True).astype(jnp.int32)
    counts = jnp.sum((flat_e[:, None] == jnp.arange(n_experts, dtype=jnp.int32)[None, :])
                     .astype(jnp.int32), axis=0)
    starts = jnp.cumsum(counts) - counts
    padded = (counts + tm - 1) // tm * tm
    pends = jnp.cumsum(padded)
    pstarts = pends - padded
    n_used = (pends[-1] // tm).astype(jnp.int32)
    blk = jnp.arange(nb, dtype=jnp.int32)
    blk_start = blk * tm
    be = jnp.minimum(jnp.sum((pends[None, :] <= blk_start[:, None]).astype(jnp.int32), axis=1),
                     n_experts - 1)
    be = jnp.where(blk < n_used, be, be[jnp.maximum(n_used - 1, 0)]).astype(jnp.int32)
    within = (blk_start - pstarts[be])[:, None] + jnp.arange(tm, dtype=jnp.int32)[None, :]
    valid = within < counts[be][:, None]
    flat = order[jnp.clip(starts[be][:, None] + within, 0, nk - 1)]
    tok = flat // TOP_K
    pad_row = nk + (blk & 1)[:, None] * tm + jnp.arange(tm, dtype=jnp.int32)[None, :]
    dst = jnp.where(valid, (flat % TOP_K) * n + tok, pad_row)
    return be, n_used.reshape(1), tok.astype(jnp.int32), dst.astype(jnp.int32)


def _combine_kernel(*refs, d, sd):
    y_refs = refs[:TOP_K]
    wt_ref, h_ref, x_ref, g2_ref, fg_ref, wgu_ref, wd_ref, o_ref = refs[TOP_K:]
    wt = wt_ref[...]
    routed = y_refs[0][...] * wt[:, 0:1]
    for k in range(1, TOP_K):
        routed = routed + y_refs[k][...] * wt[:, k:k + 1]
    hb = h_ref[...].astype(BF16)
    gu = jnp.dot(hb, wgu_ref[...], preferred_element_type=F32)
    act = (_silu(gu[:, :sd]) * gu[:, sd:]).astype(BF16)
    ffn = routed + jnp.dot(act, wd_ref[...], preferred_element_type=F32)
    x2 = x_ref[...] + g2_ref[...] * ffn
    o_ref[...] = _rms(x2, fg_ref[...])


def _combine(y, n_all, row0, wts, h2, x1, gate2, final_g, ws_gu, ws_d, tm):
    n, d = x1.shape
    sd = ws_d.shape[0]
    rb0 = row0 // tm
    kb = n_all // tm
    row = lambda w: pl.BlockSpec((tm, w), lambda i: (i, 0))
    shifted = lambda w: pl.BlockSpec((tm, w), lambda i: (rb0 + i, 0))
    y_specs = [pl.BlockSpec((tm, d), functools.partial(lambda i, k: (k * kb + rb0 + i, 0), k=k))
               for k in range(TOP_K)]
    return pl.pallas_call(
        functools.partial(_combine_kernel, d=d, sd=sd),
        grid=(n // tm,),
        in_specs=y_specs + [shifted(TOP_K), shifted(d), row(d), _mod_spec(gate2, tm, d),
                            pl.BlockSpec((1, d), lambda i: (0, 0)),
                            _resident(ws_gu.shape), _resident(ws_d.shape)],
        out_specs=row(d),
        out_shape=jax.ShapeDtypeStruct((n, d), F32),
        compiler_params=_cparams(1, 48),
        name="combine",
    )(*([y] * TOP_K), wts, h2, x1, gate2, final_g.reshape(1, d), ws_gu, ws_d)


def _in_proj(h, w_in, pw, aw, d, tm, q_dtype, want_bf16_kv):
    tn = 1024
    ident = lambda acc: (acc,)
    (u,) = _mm(h, w_in, 0, pw, [F32], ident, tm, tn, "proj_u")
    qscale = LOG2E / math.sqrt(HEAD_DIM)
    (q,) = _mm(h, w_in, pw, aw, [q_dtype], lambda acc: (acc * qscale,), tm, tn, "proj_q")
    kv_dt = [F32, BF16] if want_bf16_kv else [F32]
    both = (lambda acc: (acc, acc)) if want_bf16_kv else ident
    k = _mm(h, w_in, pw + aw, aw, kv_dt, both, tm, tn, "proj_k")
    v = _mm(h, w_in, pw + 2 * aw, aw, kv_dt, both, tm, tn, "proj_v")
    (gates,) = _mm(h, w_in, pw + 3 * aw, 2 * d, [BF16], lambda acc: (jax.nn.sigmoid(acc),),
                   tm, tn, "proj_gates")
    return u, q, k, v, gates


def kernel(x_prompt, x_sample, c_prompt, c_sample, state_pool, cache_k, cache_v, page_table,
           norm1_g, norm2_g, w_ada, b_ada, w_in, pool_mix, pool_scale, w_br_pool, w_br_attn,
           sb_bias, w_o, w_router, router_bias, w_e_gate, w_e_up, w_e_down, w_s_gate, w_s_up,
           w_s_down, final_g):
    depth = w_in.shape[0]
    nbp, seq, d = x_prompt.shape
    nbs, t, _ = x_sample.shape
    assert nbp == 1 and depth == 1 and t & (t - 1) == 0
    l = 0
    pw = pool_scale.shape[1]
    aw = w_br_attn.shape[1]
    nh = aw // HEAD_DIM
    n_experts = w_router.shape[2]
    n_pages, page = page_table.shape[1], cache_k.shape[2]
    past_len = n_pages * page
    ns = nbs * t
    n_all = seq + ns
    buf = POOL_HIST - 1

    xp = x_prompt.reshape(seq, d)
    xs = x_sample.reshape(ns, d)
    rows = nbp + nbs
    rpad = -(-rows // 8) * 8
    c_all = jnp.concatenate([c_prompt, c_sample, jnp.zeros((rpad - rows, d), F32)], axis=0)

    mod = _ada(c_all, w_ada[l], b_ada[l])
    mod_p = [mod[0:1, i * d:(i + 1) * d] for i in range(N_MOD)]
    mod_s = [jnp.repeat(mod[nbp:rows, i * d:(i + 1) * d], t, axis=0) for i in range(N_MOD)]
    wbp, wba, wo = (w.astype(BF16) for w in (w_br_pool[l], w_br_attn[l], w_o[l]))
    wr_t = w_router[l].T.astype(BF16)
    ws_gu = jnp.concatenate([w_s_gate[l], w_s_up[l]], axis=1).astype(BF16)
    ws_d = w_s_down[l].astype(BF16)
    bias2 = sb_bias[l] * LOG2E

    hp = _normmod(xp, norm1_g[l], mod_p[1], mod_p[0], 512)
    u_p, q_p, (k_p, kb_p), (v_p, vb_p), gates_p = _in_proj(hp, w_in[l], pw, aw, d, 1024, BF16, True)
    pool_p = _pool_prompt(u_p, pool_mix[l], pool_scale[l])
    attn_p = _attn_prompt(q_p, kb_p, vb_p, bias2)
    x1_p, h2 = _merge(pool_p, attn_p, gates_p, xp, mod_p[2], norm2_g[l], mod_p[4], mod_p[3],
                      wbp, wba, wo, 256, n_all, 0)

    hs = _normmod(xs, norm1_g[l], mod_s[1], mod_s[0], ns)
    u_s, q_s, (k_s,), (v_s,), gates_s = _in_proj(hs, w_in[l], pw, aw, d, ns, F32, False)
    hist_s = jnp.concatenate([jnp.zeros((nbs, 1, pw), F32), state_pool[l]], axis=1)
    pool_s = _pool_sample(u_s.reshape(nbs, t, pw), hist_s, pool_mix[l], pool_scale[l], past_len)
    attn_s = _attn_sample(q_s, k_s, v_s, cache_k[l], cache_v[l], page_table, bias2, t)
    x1_s, h2 = _merge(pool_s, attn_s, gates_s, xs, mod_s[2], norm2_g[l], mod_s[4], mod_s[3],
                      wbp, wba, wo, ns, n_all, seq, h2_prev=h2)

    ei, wt = _router(h2, wr_t, router_bias[l], 768)
    be, n_used, slot_tok, slot_dst = _dispatch_meta(ei.T, n_experts, MOE_TM)
    y = _experts(be, n_used, slot_tok, slot_dst, h2, w_e_gate[l], w_e_up[l], w_e_down[l])
    wts = wt.T
    yp = _combine(y, n_all, 0, wts, h2, x1_p, mod_p[5], final_g, ws_gu, ws_d, 128)
    ys = _combine(y, n_all, seq, wts, h2, x1_s, mod_s[5], final_g, ws_gu, ws_d, 128)

    heads = lambda a, b_: a.reshape(1, b_, -1, nh, HEAD_DIM)
    pool_state_s = jnp.concatenate([state_pool[l][:, t:], u_s.reshape(nbs, t, pw)], axis=1)
    return (yp.reshape(nbp, seq, d), ys.reshape(nbs, t, d),
            u_p[seq - buf:].reshape(1, nbp, buf, pw), heads(k_p, nbp), heads(v_p, nbp),
            pool_state_s[None], heads(k_s, nbs), heads(v_s, nbs))
```
